```python
import jax, jax.numpy as jnp
from jax import lax
import numpy as np

D_MODEL = 1024
BATCH = 4
SEQ = 4096
DEPTH = 4
DEC_BATCH = 8
DEC_SEQ = 64
PAST_LEN = 1024

CHUNK = 64
N_MIXERS = 2
N_POOL_LAYERS = (DEPTH + 1) // N_MIXERS
N_MLSTM_LAYERS = DEPTH // N_MIXERS
POOL_WINDOWS = (2, 4, 8, 16)
POOL_GROUPS = len(POOL_WINDOWS)
POOL_GROUP_WIDTH = D_MODEL // POOL_GROUPS
POOL_HIST = max(POOL_WINDOWS) - 1
MLSTM_HEADS = 4
MLSTM_DQK = D_MODEL // (2 * MLSTM_HEADS)
MLSTM_DV = D_MODEL // MLSTM_HEADS
MLSTM_QK = MLSTM_HEADS * MLSTM_DQK
MLSTM_V = MLSTM_HEADS * MLSTM_DV
MLSTM_IN = 2 * MLSTM_QK + 2 * MLSTM_V + 2 * MLSTM_HEADS
D_FF = 2816
EPS = 1e-6

kernel_name = 'pool_mlstm_macaron_stream_step'


def _rmsnorm(x, g):
    x32 = x.astype(jnp.float32)
    y = x32 * lax.rsqrt(jnp.mean(x32 * x32, axis=-1, keepdims=True) + EPS) * g.astype(jnp.float32)
    return y.astype(x.dtype)


def _swiglu(u, w_in, w_out):
    h = u @ w_in
    a, b = h[..., :D_FF], h[..., D_FF:]
    return (jax.nn.silu(a) * b) @ w_out


def _pool_mixer(u, hist, pos0, w_pool, b_pool, scale):
    B, S, _ = u.shape
    ext = jnp.concatenate([hist.astype(u.dtype), u], axis=1)
    ext32 = ext.astype(jnp.float32)
    csum = jnp.pad(jnp.cumsum(ext32, axis=1), ((0, 0), (1, 0), (0, 0)))
    pos = pos0 + jnp.arange(S)
    hi = csum[:, POOL_HIST + 1:POOL_HIST + 1 + S]
    means = []
    for g, w in enumerate(POOL_WINDOWS):
        c = slice(g * POOL_GROUP_WIDTH, (g + 1) * POOL_GROUP_WIDTH)
        lo = csum[:, POOL_HIST + 1 - w:POOL_HIST + 1 - w + S, c]
        cnt = jnp.minimum(pos + 1, w).astype(jnp.float32)
        means.append((hi[..., c] - lo) / cnt[None, :, None])
    d = jnp.concatenate(means, axis=-1) - ext32[:, POOL_HIST:]
    d = d.reshape(B, S, POOL_GROUPS, POOL_GROUP_WIDTH)
    y = jnp.einsum('bsgc,gce->bsge', d, w_pool.astype(jnp.float32)).reshape(B, S, D_MODEL)
    y = (y + b_pool.astype(jnp.float32)) * scale.astype(jnp.float32)
    return y.astype(u.dtype), ext[:, -POOL_HIST:]


def _mlstm_cell(q, k, v, log_i, log_f, C0, n0, m0):
    B, S, H, _ = q.shape
    L = min(CHUNK, S)
    NC = S // L

    def blocks(a):
        a = a.reshape((B, NC, L, H) + a.shape[3:])
        return jnp.moveaxis(a, (1, 3), (0, 2))

    causal = jnp.tril(jnp.ones((L, L), dtype=bool))

    def step(carry, xs):
        C, n, m = carry
        qc, kc, vc, ic, fc = xs
        b = jnp.cumsum(fc, axis=-1)
        log_d = jnp.where(causal, b[..., :, None] - b[..., None, :] + ic[..., None, :], -jnp.inf)
        log_state = b + m[..., None]
        m_t = jnp.maximum(jnp.max(log_d, axis=-1), log_state)
        w_intra = jnp.exp(log_d - m_t[..., None])
        w_state = jnp.exp(log_state - m_t)
        s = jnp.einsum('bhtd,bhsd->bhts', qc, kc) * w_intra
        num = jnp.einsum('bhts,bhsv->bhtv', s, vc) + w_state[..., None] * jnp.einsum('bhvd,bhtd->bhtv', C, qc)
        den = jnp.sum(s, axis=-1) + w_state * jnp.einsum('bhd,bhtd->bht', n, qc)
        h = num / jnp.maximum(jnp.abs(den), jnp.exp(-m_t))[..., None]
        m_new = m_t[..., -1]
        w_carry = jnp.exp(b[..., -1] + m - m_new)
        w_write = jnp.exp(b[..., -1:] - b + ic - m_new[..., None])
        C_new = w_carry[..., None, None] * C + jnp.einsum('bhsv,bhsd->bhvd', vc * w_write[..., None], kc)
        n_new = w_carry[..., None] * n + jnp.einsum('bhs,bhsd->bhd', w_write, kc)
        return (C_new, n_new, m_new), h

    init = (C0.astype(jnp.float32), n0.astype(jnp.float32), m0.astype(jnp.float32))
    xs = (blocks(q), blocks(k), blocks(v), blocks(log_i), blocks(log_f))
    (C, n, m), h = lax.scan(step, init, xs)
    h = jnp.moveaxis(h, (0, 2), (1, 3)).reshape(B, S, H, MLSTM_DV)
    return h, (C, n, m)


def _mlstm_mixer(u, C0, n0, m0, w_in, b_i, b_f, g_norm, w_out):
    B, S, _ = u.shape
    p = u @ w_in
    o0 = 2 * MLSTM_QK + MLSTM_V
    g0 = o0 + MLSTM_V
    q = p[..., :MLSTM_QK].astype(jnp.float32).reshape(B, S, MLSTM_HEADS, MLSTM_DQK) * (MLSTM_DQK ** -0.5)
    k = p[..., MLSTM_QK:2 * MLSTM_QK].astype(jnp.float32).reshape(B, S, MLSTM_HEADS, MLSTM_DQK)
    v = p[..., 2 * MLSTM_QK:o0].astype(jnp.float32).reshape(B, S, MLSTM_HEADS, MLSTM_DV)
    o = p[..., o0:g0].astype(jnp.float32)
    log_i = p[..., g0:g0 + MLSTM_HEADS].astype(jnp.float32) + b_i.astype(jnp.float32)
    log_f = jax.nn.log_sigmoid(p[..., g0 + MLSTM_HEADS:].astype(jnp.float32) + b_f.astype(jnp.float32))
    h, state = _mlstm_cell(q, k, v, log_i, log_f, C0, n0, m0)
    h = h * lax.rsqrt(jnp.mean(h * h, axis=-1, keepdims=True) + EPS)
    h = h.reshape(B, S, MLSTM_V) * g_norm.astype(jnp.float32)
    y = (jax.nn.sigmoid(o) * h).astype(u.dtype) @ w_out
    return y, state


def _trunk(x, pool_hist, mC, mn, mm, pos0, norm_ffn1, ffn1_w_in, ffn1_w_out, norm_mix, pool_w, pool_b,
           pool_scale, mlstm_w_in, mlstm_b_i, mlstm_b_f, mlstm_norm, mlstm_w_out, norm_ffn2, ffn2_w_in,
           ffn2_w_out, norm_final):
    pool_new, C_new, n_new, m_new = [], [], [], []
    for i in range(DEPTH):
        j = i // N_MIXERS
        x = x + 0.5 * _swiglu(_rmsnorm(x, norm_ffn1[i]), ffn1_w_in[i], ffn1_w_out[i])
        u = _rmsnorm(x, norm_mix[i])
        if i % N_MIXERS == 0:
            y, hist = _pool_mixer(u, pool_hist[j], pos0, pool_w[j], pool_b[j], pool_scale[j])
            pool_new.append(hist.astype(pool_hist.dtype))
        else:
            y, (C, n, m) = _mlstm_mixer(u, mC[j], mn[j], mm[j], mlstm_w_in[j], mlstm_b_i[j], mlstm_b_f[j],
                                        mlstm_norm[j], mlstm_w_out[j])
            C_new.append(C.astype(mC.dtype))
            n_new.append(n.astype(mn.dtype))
            m_new.append(m.astype(mm.dtype))
        x = x + y
        x = x + 0.5 * _swiglu(_rmsnorm(x, norm_ffn2[i]), ffn2_w_in[i], ffn2_w_out[i])
    return _rmsnorm(x, norm_final), jnp.stack(pool_new), jnp.stack(C_new), jnp.stack(n_new), jnp.stack(m_new)


def setup_inputs(seed: int = 0) -> dict:
    key = jax.random.key(seed)
    ks = jax.random.split(key, 24)
    f32 = jnp.float32

    def nrm(k, shape, scale):
        return jax.random.normal(k, shape, f32) * scale

    b_f = jnp.broadcast_to(jnp.linspace(3.0, 6.0, MLSTM_HEADS, dtype=f32), (N_MLSTM_LAYERS, MLSTM_HEADS))
    return {
        'x_prompt': nrm(ks[0], (BATCH, SEQ, D_MODEL), 1.0),
        'x_sample': nrm(ks[1], (DEC_BATCH, DEC_SEQ, D_MODEL), 1.0),
        'state_pool': nrm(ks[2], (N_POOL_LAYERS, DEC_BATCH, POOL_HIST, D_MODEL), 1.0),
        'state_mlstm_C': nrm(ks[3], (N_MLSTM_LAYERS, DEC_BATCH, MLSTM_HEADS, MLSTM_DV, MLSTM_DQK), 0.1),
        'state_mlstm_n': nrm(ks[4], (N_MLSTM_LAYERS, DEC_BATCH, MLSTM_HEADS, MLSTM_DQK), 0.1),
        'state_mlstm_m': nrm(ks[5], (N_MLSTM_LAYERS, DEC_BATCH, MLSTM_HEADS), 0.5),
        'norm_ffn1': 1.0 + nrm(ks[6], (DEPTH, D_MODEL), 0.05),
        'ffn1_w_in': nrm(ks[7], (DEPTH, D_MODEL, 2 * D_FF), D_MODEL ** -0.5),
        'ffn1_w_out': nrm(ks[8], (DEPTH, D_FF, D_MODEL), D_FF ** -0.5),
        'norm_mix': 1.0 + nrm(ks[9], (DEPTH, D_MODEL), 0.05),
        'pool_w': nrm(ks[10], (N_POOL_LAYERS, POOL_GROUPS, POOL_GROUP_WIDTH, POOL_GROUP_WIDTH), POOL_GROUP_WIDTH ** -0.5),
        'pool_b': nrm(ks[11], (N_POOL_LAYERS, D_MODEL), 0.02),
        'pool_scale': 1.0 + nrm(ks[12], (N_POOL_LAYERS, D_MODEL), 0.1),
        'mlstm_w_in': nrm(ks[13], (N_MLSTM_LAYERS, D_MODEL, MLSTM_IN), D_MODEL ** -0.5),
        'mlstm_b_i': nrm(ks[14], (N_MLSTM_LAYERS, MLSTM_HEADS), 0.1),
        'mlstm_b_f': b_f + nrm(ks[15], (N_MLSTM_LAYERS, MLSTM_HEADS), 0.1),
        'mlstm_norm': 1.0 + nrm(ks[16], (N_MLSTM_LAYERS, MLSTM_V), 0.05),
        'mlstm_w_out': nrm(ks[17], (N_MLSTM_LAYERS, MLSTM_V, D_MODEL), MLSTM_V ** -0.5),
        'norm_ffn2': 1.0 + nrm(ks[18], (DEPTH, D_MODEL), 0.05),
        'ffn2_w_in': nrm(ks[19], (DEPTH, D_MODEL, 2 * D_FF), D_MODEL ** -0.5),
        'ffn2_w_out': nrm(ks[20], (DEPTH, D_FF, D_MODEL), D_FF ** -0.5),
        'norm_final': 1.0 + nrm(ks[21], (D_MODEL,), 0.05),
    }


def reference(x_prompt, x_sample, state_pool, state_mlstm_C, state_mlstm_n, state_mlstm_m, norm_ffn1,
              ffn1_w_in, ffn1_w_out, norm_mix, pool_w, pool_b, pool_scale, mlstm_w_in, mlstm_b_i, mlstm_b_f,
              mlstm_norm, mlstm_w_out, norm_ffn2, ffn2_w_in, ffn2_w_out, norm_final):
    weights = (norm_ffn1, ffn1_w_in, ffn1_w_out, norm_mix, pool_w, pool_b, pool_scale, mlstm_w_in, mlstm_b_i,
               mlstm_b_f, mlstm_norm, mlstm_w_out, norm_ffn2, ffn2_w_in, ffn2_w_out, norm_final)
    B = x_prompt.shape[0]
    zero_pool = jnp.zeros((N_POOL_LAYERS, B, POOL_HIST, D_MODEL), state_pool.dtype)
    zero_C = jnp.zeros((N_MLSTM_LAYERS, B, MLSTM_HEADS, MLSTM_DV, MLSTM_DQK), state_mlstm_C.dtype)
    zero_n = jnp.zeros((N_MLSTM_LAYERS, B, MLSTM_HEADS, MLSTM_DQK), state_mlstm_n.dtype)
    zero_m = jnp.zeros((N_MLSTM_LAYERS, B, MLSTM_HEADS), state_mlstm_m.dtype)
    y_prompt, pool_p, C_p, n_p, m_p = _trunk(x_prompt, zero_pool, zero_C, zero_n, zero_m, 0, *weights)
    y_sample, pool_s, C_s, n_s, m_s = _trunk(x_sample, state_pool, state_mlstm_C, state_mlstm_n,
                                             state_mlstm_m, PAST_LEN, *weights)
    return (y_prompt, y_sample, pool_p, C_p, n_p, m_p, pool_s, C_s, n_s, m_s)
```

```python
import functools

import jax
import jax.numpy as jnp
from jax import lax
from jax.experimental import pallas as pl
from jax.experimental.pallas import tpu as pltpu

D_MODEL = 1024
D_FF = 2816
DEPTH = 4
PAST_LEN = 1024
CHUNK = 64
POOL_WINDOWS = (2, 4, 8, 16)
POOL_GROUP_WIDTH = D_MODEL // len(POOL_WINDOWS)
POOL_HIST = max(POOL_WINDOWS) - 1
HIST_ROWS = POOL_HIST + 1
MLSTM_HEADS = 4
MLSTM_DQK = D_MODEL // (2 * MLSTM_HEADS)
MLSTM_DV = D_MODEL // MLSTM_HEADS
MLSTM_QK = MLSTM_HEADS * MLSTM_DQK
MLSTM_V = MLSTM_HEADS * MLSTM_DV
EPS = 1e-6

VMEM_LIMIT_BYTES = 56 * 1024 * 1024

F32 = jnp.float32
BF16 = jnp.bfloat16


def _rmsnorm_f32(x, g):
    return x * lax.rsqrt(jnp.mean(x * x, axis=-1, keepdims=True) + EPS) * g


def _resident(shape):
    nd = len(shape)
    return pl.BlockSpec(shape, lambda *_: (0,) * nd, pipeline_mode=pl.Buffered(1))


def _ffn_body(x_ref, g_ref, wa_ref, wb_ref, wo_ref, gf_ref, o_ref, *, final_norm):
    x = x_ref[...]
    xn = _rmsnorm_f32(x, g_ref[...]).astype(BF16)
    a = jnp.dot(xn, wa_ref[...], preferred_element_type=F32)
    b = jnp.dot(xn, wb_ref[...], preferred_element_type=F32)
    h = (a * jax.nn.sigmoid(a) * b).astype(BF16)
    y = x + 0.5 * jnp.dot(h, wo_ref[...], preferred_element_type=F32)
    if final_norm:
        y = _rmsnorm_f32(y, gf_ref[...])
    o_ref[...] = y


def _ffn(x, g, w_in, w_out, g_final, *, final_norm, tm):
    T = x.shape[0]
    assert T % tm == 0
    row = pl.BlockSpec((tm, D_MODEL), lambda i: (i, 0))
    return pl.pallas_call(
        functools.partial(_ffn_body, final_norm=final_norm),
        grid=(T // tm,),
        in_specs=[
            row,
            _resident((1, D_MODEL)),
            pl.BlockSpec((D_MODEL, D_FF), lambda i: (0, 0), pipeline_mode=pl.Buffered(1)),
            pl.BlockSpec((D_MODEL, D_FF), lambda i: (0, 1), pipeline_mode=pl.Buffered(1)),
            _resident((D_FF, D_MODEL)),
            _resident((1, D_MODEL)),
        ],
        out_specs=row,
        out_shape=jax.ShapeDtypeStruct((T, D_MODEL), F32),
        compiler_params=pltpu.CompilerParams(
            dimension_semantics=("arbitrary",), vmem_limit_bytes=VMEM_LIMIT_BYTES),
        name="ffn",
    )(x, g, w_in, w_in, w_out, g_final)


def _pool_body(x_ref, hist_ref, g_ref, w_ref, b_ref, sc_ref, o_ref, hist_out_ref, ext_ref,
               *, ts, pos0, n_tiles):
    t = pl.program_id(1)
    x = x_ref[0]
    u = _rmsnorm_f32(x, g_ref[...])

    @pl.when(t == 0)
    def _():
        ext_ref[0:HIST_ROWS, :] = hist_ref[0]

    ext_ref[HIST_ROWS:HIST_ROWS + ts, :] = u
    ext = ext_ref[...]
    pos = pos0 + t * ts + lax.broadcasted_iota(jnp.int32, (ts, 1), 0)
    ys = []
    for gi, w in enumerate(POOL_WINDOWS):
        c0 = gi * POOL_GROUP_WIDTH
        s = ext[:, c0:c0 + POOL_GROUP_WIDTH]
        k = 1
        while k < w:
            s = s + pltpu.roll(s, k, 0)
            k *= 2
        cnt = jnp.minimum(pos + 1, w).astype(F32)
        d = s[HIST_ROWS:, :] / cnt - u[:, c0:c0 + POOL_GROUP_WIDTH]
        ys.append(jnp.dot(d.astype(BF16), w_ref[gi], preferred_element_type=F32))
    y = (jnp.concatenate(ys, axis=-1) + b_ref[...]) * sc_ref[...]
    o_ref[0] = x + y
    tail = ext_ref[ts:ts + HIST_ROWS, :]
    ext_ref[0:HIST_ROWS, :] = tail

    @pl.when(t == n_tiles - 1)
    def _():
        hist_out_ref[0] = tail


def _pool_mixer(x, hist, g, w, b, sc, *, pos0, ts):
    B, S, _ = x.shape
    assert S % ts == 0 and ts >= HIST_ROWS
    n_tiles = S // ts
    return pl.pallas_call(
        functools.partial(_pool_body, ts=ts, pos0=pos0, n_tiles=n_tiles),
        grid=(B, n_tiles),
        in_specs=[
            pl.BlockSpec((1, ts, D_MODEL), lambda bi, t: (bi, t, 0)),
            pl.BlockSpec((1, HIST_ROWS, D_MODEL), lambda bi, t: (bi, 0, 0)),
            _resident((1, D_MODEL)),
            _resident(w.shape),
            _resident((1, D_MODEL)),
            _resident((1, D_MODEL)),
        ],
        out_specs=[
            pl.BlockSpec((1, ts, D_MODEL), lambda bi, t: (bi, t, 0)),
            pl.BlockSpec((1, HIST_ROWS, D_MODEL), lambda bi, t: (bi, 0, 0)),
        ],
        out_shape=[
            jax.ShapeDtypeStruct((B, S, D_MODEL), F32),
            jax.ShapeDtypeStruct((B, HIST_ROWS, D_MODEL), F32),
        ],
        scratch_shapes=[pltpu.VMEM((HIST_ROWS + ts, D_MODEL), F32)],
        compiler_params=pltpu.CompilerParams(
            dimension_semantics=("arbitrary", "arbitrary"), vmem_limit_bytes=VMEM_LIMIT_BYTES),
        name="pool_mixer",
    )(x, hist, g, w, b, sc)


def _mlstm_cell_jnp(q, k, v, log_i, log_f, C0, n0, m0):
    B, S, H, _ = q.shape
    L = min(CHUNK, S)
    NC = S // L

    def blocks(a):
        a = a.reshape((B, NC, L, H) + a.shape[3:])
        return jnp.moveaxis(a, (1, 3), (0, 2))

    causal = jnp.tril(jnp.ones((L, L), dtype=bool))

    def step(carry, xs):
        C, n, m = carry
        qc, kc, vc, ic, fc = xs
        b = jnp.cumsum(fc, axis=-1)
        log_d = jnp.where(causal, b[..., :, None] - b[..., None, :] + ic[..., None, :], -jnp.inf)
        log_state = b + m[..., None]
        m_t = jnp.maximum(jnp.max(log_d, axis=-1), log_state)
        w_intra = jnp.exp(log_d - m_t[..., None])
        w_state = jnp.exp(log_state - m_t)
        s = jnp.einsum('bhtd,bhsd->bhts', qc, kc) * w_intra
        num = jnp.einsum('bhts,bhsv->bhtv', s, vc) + w_state[..., None] * jnp.einsum('bhvd,bhtd->bhtv', C, qc)
        den = jnp.sum(s, axis=-1) + w_state * jnp.einsum('bhd,bhtd->bht', n, qc)
        h = num / jnp.maximum(jnp.abs(den), jnp.exp(-m_t))[..., None]
        m_new = m_t[..., -1]
        w_carry = jnp.exp(b[..., -1] + m - m_new)
        w_write = jnp.exp(b[..., -1:] - b + ic - m_new[..., None])
        C_new = w_carry[..., None, None] * C + jnp.einsum('bhsv,bhsd->bhvd', vc * w_write[..., None], kc)
        n_new = w_carry[..., None] * n + jnp.einsum('bhs,bhsd->bhd', w_write, kc)
        return (C_new, n_new, m_new), h

    (C, n, m), h = lax.scan(step, (C0, n0, m0), (blocks(q), blocks(k), blocks(v), blocks(log_i), blocks(log_f)))
    h = jnp.moveaxis(h, (0, 2), (1, 3)).reshape(B, S, H, MLSTM_DV)
    return h, (C, n, m)


def _mlstm_mixer_jnp(x, C0, n0, m0, g, w_in, b_i, b_f, g_norm, w_out):
    B, S, _ = x.shape
    u = _rmsnorm_f32(x, g)
    p = u @ w_in
    o0 = 2 * MLSTM_QK + MLSTM_V
    g0 = o0 + MLSTM_V
    q = p[..., :MLSTM_QK].reshape(B, S, MLSTM_HEADS, MLSTM_DQK) * (MLSTM_DQK ** -0.5)
    k = p[..., MLSTM_QK:2 * MLSTM_QK].reshape(B, S, MLSTM_HEADS, MLSTM_DQK)
    v = p[..., 2 * MLSTM_QK:o0].reshape(B, S, MLSTM_HEADS, MLSTM_DV)
    o = p[..., o0:g0]
    log_i = p[..., g0:g0 + MLSTM_HEADS] + b_i
    log_f = jax.nn.log_sigmoid(p[..., g0 + MLSTM_HEADS:] + b_f)
    h, state = _mlstm_cell_jnp(q, k, v, log_i, log_f, C0, n0, m0)
    h = h * lax.rsqrt(jnp.mean(h * h, axis=-1, keepdims=True) + EPS)
    h = h.reshape(B, S, MLSTM_V) * g_norm
    y = (jax.nn.sigmoid(o) * h) @ w_out
    return x + y, state


def _trunk(x, pool_hist, mC, mn, mm, pos0, W, *, tm, ts):
    B, S, _ = x.shape
    pool_new, C_new, n_new, m_new = [], [], [], []
    x2 = x.reshape(B * S, D_MODEL)
    for i in range(DEPTH):
        j = i // 2
        x2 = _ffn(x2, W['norm_ffn1'][i], W['ffn1_w_in'][i], W['ffn1_w_out'][i], W['norm_final'],
                  final_norm=False, tm=tm)
        x3 = x2.reshape(B, S, D_MODEL)
        if i % 2 == 0:
            hist = jnp.pad(pool_hist[j], ((0, 0), (1, 0), (0, 0)))
            x3, hist_new = _pool_mixer(x3, hist, W['norm_mix'][i], W['pool_w'][j], W['pool_b'][j],
                                       W['pool_scale'][j], pos0=pos0, ts=ts)
            pool_new.append(hist_new[:, 1:])
        else:
            x3, (C, n, m) = _mlstm_mixer_jnp(x3, mC[j], mn[j], mm[j], W['norm_mix'][i], W['mlstm_w_in'][j],
                                             W['mlstm_b_i'][j], W['mlstm_b_f'][j], W['mlstm_norm'][j],
                                             W['mlstm_w_out'][j])
            C_new.append(C)
            n_new.append(n)
            m_new.append(m)
        x2 = x3.reshape(B * S, D_MODEL)
        x2 = _ffn(x2, W['norm_ffn2'][i], W['ffn2_w_in'][i], W['ffn2_w_out'][i], W['norm_final'],
                  final_norm=(i == DEPTH - 1), tm=tm)
    return (x2.reshape(B, S, D_MODEL), jnp.stack(pool_new), jnp.stack(C_new), jnp.stack(n_new),
            jnp.stack(m_new))


def kernel(x_prompt, x_sample, state_pool, state_mlstm_C, state_mlstm_n, state_mlstm_m, norm_ffn1,
           ffn1_w_in, ffn1_w_out, norm_mix, pool_w, pool_b, pool_scale, mlstm_w_in, mlstm_b_i, mlstm_b_f,
           mlstm_norm, mlstm_w_out, norm_ffn2, ffn2_w_in, ffn2_w_out, norm_final):
    W = dict(
        norm_ffn1=norm_ffn1[:, None, :], norm_ffn2=norm_ffn2[:, None, :], norm_mix=norm_mix[:, None, :],
        norm_final=norm_final[None, :],
        ffn1_w_in=ffn1_w_in.astype(BF16), ffn1_w_out=ffn1_w_out.astype(BF16),
        ffn2_w_in=ffn2_w_in.astype(BF16), ffn2_w_out=ffn2_w_out.astype(BF16),
        pool_w=pool_w.astype(BF16), pool_b=pool_b[:, None, :], pool_scale=pool_scale[:, None, :],
        mlstm_w_in=mlstm_w_in, mlstm_b_i=mlstm_b_i, mlstm_b_f=mlstm_b_f, mlstm_norm=mlstm_norm,
        mlstm_w_out=mlstm_w_out,
    )
    B = x_prompt.shape[0]
    zero_pool = jnp.zeros((state_pool.shape[0], B) + state_pool.shape[2:], state_pool.dtype)
    zero_C = jnp.zeros((state_mlstm_C.shape[0], B) + state_mlstm_C.shape[2:], state_mlstm_C.dtype)
    zero_n = jnp.zeros((state_mlstm_n.shape[0], B) + state_mlstm_n.shape[2:], state_mlstm_n.dtype)
    zero_m = jnp.zeros((state_mlstm_m.shape[0], B) + state_mlstm_m.shape[2:], state_mlstm_m.dtype)
    y_p, pool_p, C_p, n_p, m_p = _trunk(x_prompt, zero_pool, zero_C, zero_n, zero_m, 0, W, tm=512, ts=512)
    y_s, pool_s, C_s, n_s, m_s = _trunk(x_sample, state_pool, state_mlstm_C, state_mlstm_n, state_mlstm_m,
                                        PAST_LEN, W, tm=512, ts=64)
    return (y_p, y_s, pool_p, C_p, n_p, m_p, pool_s, C_s, n_s, m_s)
```

```python
import functools

import jax
import jax.numpy as jnp
from jax import lax
from jax.experimental import pallas as pl
from jax.experimental.pallas import tpu as pltpu

D_MODEL = 1024
D_FF = 2816
DEPTH = 4
PAST_LEN = 1024
POOL_WINDOWS = (2, 4, 8, 16)
POOL_GROUP_WIDTH = D_MODEL // len(POOL_WINDOWS)
POOL_HIST = max(POOL_WINDOWS) - 1
HIST_ROWS = POOL_HIST + 1
MLSTM_HEADS = 4
MLSTM_DQK = D_MODEL // (2 * MLSTM_HEADS)
MLSTM_DV = D_MODEL // MLSTM_HEADS
MLSTM_QK = MLSTM_HEADS * MLSTM_DQK
MLSTM_V = MLSTM_HEADS * MLSTM_DV
MLSTM_MAIN = 2 * MLSTM_QK + 2 * MLSTM_V
GATE_LANES = 128
EPS = 1e-6

VMEM_LIMIT_BYTES = 56 * 1024 * 1024

F32 = jnp.float32
BF16 = jnp.bfloat16


def _rmsnorm_f32(x, g):
    return x * lax.rsqrt(jnp.mean(x * x, axis=-1, keepdims=True) + EPS) * g


def _resident(shape):
    nd = len(shape)
    return pl.BlockSpec(shape, lambda *_: (0,) * nd, pipeline_mode=pl.Buffered(1))


def _ffn_body(x_ref, g_ref, wa_ref, wb_ref, wo_ref, gf_ref, o_ref, *, final_norm):
    x = x_ref[...]
    xn = _rmsnorm_f32(x, g_ref[...]).astype(BF16)
    a = jnp.dot(xn, wa_ref[...], preferred_element_type=F32)
    b = jnp.dot(xn, wb_ref[...], preferred_element_type=F32)
    h = (a * jax.nn.sigmoid(a) * b).astype(BF16)
    y = x + 0.5 * jnp.dot(h, wo_ref[...], preferred_element_type=F32)
    if final_norm:
        y = _rmsnorm_f32(y, gf_ref[...])
    o_ref[...] = y


def _ffn(x, g, w_in, w_out, g_final, *, final_norm, tm):
    T = x.shape[0]
    assert T % tm == 0
    row = pl.BlockSpec((tm, D_MODEL), lambda i: (i, 0))
    return pl.pallas_call(
        functools.partial(_ffn_body, final_norm=final_norm),
        grid=(T // tm,),
        in_specs=[
            row,
            _resident((1, D_MODEL)),
            pl.BlockSpec((D_MODEL, D_FF), lambda i: (0, 0), pipeline_mode=pl.Buffered(1)),
            pl.BlockSpec((D_MODEL, D_FF), lambda i: (0, 1), pipeline_mode=pl.Buffered(1)),
            _resident((D_FF, D_MODEL)),
            _resident((1, D_MODEL)),
        ],
        out_specs=row,
        out_shape=jax.ShapeDtypeStruct((T, D_MODEL), F32),
        compiler_params=pltpu.CompilerParams(
            dimension_semantics=("arbitrary",), vmem_limit_bytes=VMEM_LIMIT_BYTES),
        name="ffn",
    )(x, g, w_in, w_in, w_out, g_final)


def _pool_body(x_ref, hist_ref, g_ref, w_ref, b_ref, sc_ref, o_ref, hist_out_ref, ext_ref,
               *, ts, pos0, n_tiles):
    t = pl.program_id(1)
    x = x_ref[0]
    u = _rmsnorm_f32(x, g_ref[...])

    @pl.when(t == 0)
    def _():
        ext_ref[0:HIST_ROWS, :] = hist_ref[0]

    ext_ref[HIST_ROWS:HIST_ROWS + ts, :] = u
    ext = ext_ref[...]
    pos = pos0 + t * ts + lax.broadcasted_iota(jnp.int32, (ts, 1), 0)
    ys = []
    for gi, w in enumerate(POOL_WINDOWS):
        c0 = gi * POOL_GROUP_WIDTH
        s = ext[:, c0:c0 + POOL_GROUP_WIDTH]
        k = 1
        while k < w:
            s = s + pltpu.roll(s, k, 0)
            k *= 2
        cnt = jnp.minimum(pos + 1, w).astype(F32)
        d = s[HIST_ROWS:, :] / cnt - u[:, c0:c0 + POOL_GROUP_WIDTH]
        ys.append(jnp.dot(d.astype(BF16), w_ref[gi], preferred_element_type=F32))
    y = (jnp.concatenate(ys, axis=-1) + b_ref[...]) * sc_ref[...]
    o_ref[0] = x + y
    tail = ext_ref[ts:ts + HIST_ROWS, :]
    ext_ref[0:HIST_ROWS, :] = tail

    @pl.when(t == n_tiles - 1)
    def _():
        hist_out_ref[0] = tail


def _pool_mixer(x, hist, g, w, b, sc, *, pos0, ts):
    B, S, _ = x.shape
    assert S % ts == 0 and ts >= HIST_ROWS
    n_tiles = S // ts
    return pl.pallas_call(
        functools.partial(_pool_body, ts=ts, pos0=pos0, n_tiles=n_tiles),
        grid=(B, n_tiles),
        in_specs=[
            pl.BlockSpec((1, ts, D_MODEL), lambda bi, t: (bi, t, 0)),
            pl.BlockSpec((1, HIST_ROWS, D_MODEL), lambda bi, t: (bi, 0, 0)),
            _resident((1, D_MODEL)),
            _resident(w.shape),
            _resident((1, D_MODEL)),
            _resident((1, D_MODEL)),
        ],
        out_specs=[
            pl.BlockSpec((1, ts, D_MODEL), lambda bi, t: (bi, t, 0)),
            pl.BlockSpec((1, HIST_ROWS, D_MODEL), lambda bi, t: (bi, 0, 0)),
        ],
        out_shape=[
            jax.ShapeDtypeStruct((B, S, D_MODEL), F32),
            jax.ShapeDtypeStruct((B, HIST_ROWS, D_MODEL), F32),
        ],
        scratch_shapes=[pltpu.VMEM((HIST_ROWS + ts, D_MODEL), F32)],
        compiler_params=pltpu.CompilerParams(
            dimension_semantics=("arbitrary", "arbitrary"), vmem_limit_bytes=VMEM_LIMIT_BYTES),
        name="pool_mixer",
    )(x, hist, g, w, b, sc)


def _log_sigmoid(x):
    return jnp.minimum(x, 0.0) - jnp.log1p(jnp.exp(-jnp.abs(x)))


def _dot_nt(a, b):
    return lax.dot_general(a, b, (((1,), (1,)), ((), ())), preferred_element_type=F32)


def _dot_tn(a, b):
    return lax.dot_general(a, b, (((0,), (0,)), ((), ())), preferred_element_type=F32)


def _mlstm_body(x_ref, c0_ref, n0_ref, m0_ref, g_ref, win_ref, wg_ref, bg_ref, gn_ref, wo_ref,
                o_ref, c_ref, n_ref, m_ref, p_s, gate_s, hn_s, *, tl, L, per_chunk_state):
    H = MLSTM_HEADS
    t = pl.program_id(1)

    @pl.when(t == 0)
    def _():
        c_ref[...] = c0_ref[...]
        n_ref[...] = n0_ref[...]
        m_ref[...] = m0_ref[...]

    x = x_ref[...]
    u = _rmsnorm_f32(x, g_ref[...]).astype(BF16)
    p_s[...] = jnp.dot(u, win_ref[...], preferred_element_type=F32)
    pre = jnp.dot(u, wg_ref[...], preferred_element_type=F32) + bg_ref[...]
    lane = lax.broadcasted_iota(jnp.int32, pre.shape, 1)
    gate_s[...] = jnp.where(lane < H, pre, _log_sigmoid(pre))

    row_g = lax.broadcasted_iota(jnp.int32, (L, GATE_LANES), 0)
    causal = lax.broadcasted_iota(jnp.int32, (L, L), 0) >= lax.broadcasted_iota(jnp.int32, (L, L), 1)
    for c in range(tl // L):
        r0 = c * L
        si = c if per_chunk_state else 0
        G = gate_s[r0:r0 + L, :]
        Bc = G
        k = 1
        while k < L:
            Bc = Bc + jnp.where(row_g >= k, pltpu.roll(Bc, k, 0), 0.0)
            k *= 2
        A = G - pltpu.roll(Bc, GATE_LANES - H, 1)
        AT = A.T
        for h in range(H):
            bcol = Bc[:, H + h:H + h + 1]
            acol = A[:, h:h + 1]
            arow = AT[h:h + 1, :]
            C = c_ref[si, h]
            n = n_ref[si, h]
            m_prev = m_ref[si, h][:, 0:1]
            q = p_s[r0:r0 + L, h * MLSTM_DQK:(h + 1) * MLSTM_DQK] * (MLSTM_DQK ** -0.5)
            kk = p_s[r0:r0 + L, MLSTM_QK + h * MLSTM_DQK:MLSTM_QK + (h + 1) * MLSTM_DQK]
            v0 = 2 * MLSTM_QK + h * MLSTM_DV
            v = p_s[r0:r0 + L, v0:v0 + MLSTM_DV]
            og = p_s[r0:r0 + L, v0 + MLSTM_V:v0 + MLSTM_V + MLSTM_DV]
            qb = q.astype(BF16)
            kb = kk.astype(BF16)

            log_d = jnp.where(causal, bcol + arow, -jnp.inf)
            log_state = bcol + m_prev
            m_t = jnp.maximum(jnp.max(log_d, axis=-1, keepdims=True), log_state)
            w_intra = jnp.exp(log_d - m_t)
            w_state = jnp.exp(log_state - m_t)
            s = _dot_nt(qb, kb) * w_intra
            num = (jnp.dot(s.astype(BF16), v.astype(BF16), preferred_element_type=F32)
                   + w_state * _dot_nt(qb, C.astype(BF16)))
            den = jnp.sum(s, axis=-1, keepdims=True) + w_state * jnp.sum(q * n, axis=-1, keepdims=True)
            hh = num / jnp.maximum(jnp.abs(den), jnp.exp(-m_t))
            hh = hh * lax.rsqrt(jnp.mean(hh * hh, axis=-1, keepdims=True) + EPS)
            hh = hh * gn_ref[:, h * MLSTM_DV:(h + 1) * MLSTM_DV]
            hn_s[r0:r0 + L, h * MLSTM_DV:(h + 1) * MLSTM_DV] = (jax.nn.sigmoid(og) * hh).astype(BF16)

            m_new = m_t[L - 1:L, :]
            b_last = bcol[L - 1:L, :]
            w_carry = jnp.exp(b_last + m_prev - m_new)
            w_write = jnp.exp(b_last + acol - m_new)
            c_ref[si, h] = w_carry * C + _dot_tn((v * w_write).astype(BF16), kb)
            n_ref[si, h] = w_carry * n + jnp.sum(kk * w_write, axis=0, keepdims=True)
            m_ref[si, h] = jnp.broadcast_to(m_new, (1, GATE_LANES))

    o_ref[...] = x + jnp.dot(hn_s[...], wo_ref[...], preferred_element_type=F32)


def _mlstm_mixer(x, C0, n0, m0, g, w_main, w_gate, b_gate, g_norm, w_out, *, seq, tl, L, per_chunk_state):
    T = x.shape[0]
    H = MLSTM_HEADS
    B = C0.shape[0]
    nb = B if per_chunk_state else 1
    tiles_per_seq = 1 if per_chunk_state else seq // tl
    if per_chunk_state:
        assert seq == L and tl == nb * L and T == tl
    else:
        assert seq % tl == 0 and tl % L == 0 and T == B * seq
    row = pl.BlockSpec((tl, D_MODEL), lambda b, t: (b * tiles_per_seq + t, 0))
    c_spec = pl.BlockSpec((nb, H, MLSTM_DV, MLSTM_DQK), lambda b, t: (b, 0, 0, 0))
    n_spec = pl.BlockSpec((nb, H, 1, MLSTM_DQK), lambda b, t: (b, 0, 0, 0))
    m_spec = pl.BlockSpec((nb, H, 1, GATE_LANES), lambda b, t: (b, 0, 0, 0))
    return pl.pallas_call(
        functools.partial(_mlstm_body, tl=tl, L=L, per_chunk_state=per_chunk_state),
        grid=(B // nb, tiles_per_seq),
        in_specs=[
            row, c_spec, n_spec, m_spec,
            _resident((1, D_MODEL)),
            _resident((D_MODEL, MLSTM_MAIN)),
            _resident((D_MODEL, GATE_LANES)),
            _resident((1, GATE_LANES)),
            _resident((1, MLSTM_V)),
            _resident((MLSTM_V, D_MODEL)),
        ],
        out_specs=[row, c_spec, n_spec, m_spec],
        out_shape=[
            jax.ShapeDtypeStruct((T, D_MODEL), F32),
            jax.ShapeDtypeStruct((B, H, MLSTM_DV, MLSTM_DQK), F32),
            jax.ShapeDtypeStruct((B, H, 1, MLSTM_DQK), F32),
            jax.ShapeDtypeStruct((B, H, 1, GATE_LANES), F32),
        ],
        scratch_shapes=[
            pltpu.VMEM((tl, MLSTM_MAIN), F32),
            pltpu.VMEM((tl, GATE_LANES), F32),
            pltpu.VMEM((tl, MLSTM_V), BF16),
        ],
        compiler_params=pltpu.CompilerParams(
            dimension_semantics=("arbitrary", "arbitrary"), vmem_limit_bytes=VMEM_LIMIT_BYTES),
        name="mlstm_layer",
    )(x, C0, n0, m0, g, w_main, w_gate, b_gate, g_norm, w_out)


def _trunk(x, pool_hist, mC, mn, mm, pos0, W, *, tm, ts, tl, L, per_chunk_state):
    B, S, _ = x.shape
    pool_new, C_new, n_new, m_new = [], [], [], []
    x2 = x.reshape(B * S, D_MODEL)
    for i in range(DEPTH):
        j = i // 2
        x2 = _ffn(x2, W['norm_ffn1'][i], W['ffn1_w_in'][i], W['ffn1_w_out'][i], W['norm_final'],
                  final_norm=False, tm=tm)
        if i % 2 == 0:
            hist = jnp.pad(pool_hist[j], ((0, 0), (1, 0), (0, 0)))
            x3, hist_new = _pool_mixer(x2.reshape(B, S, D_MODEL), hist, W['norm_mix'][i], W['pool_w'][j],
                                       W['pool_b'][j], W['pool_scale'][j], pos0=pos0, ts=ts)
            x2 = x3.reshape(B * S, D_MODEL)
            pool_new.append(hist_new[:, 1:])
        else:
            m0 = jnp.broadcast_to(mm[j][:, :, None, None], (B, MLSTM_HEADS, 1, GATE_LANES))
            x2, C, n, m = _mlstm_mixer(x2, mC[j], mn[j][:, :, None, :], m0, W['norm_mix'][i],
                                       W['mlstm_w_main'][j], W['mlstm_w_gate'][j], W['mlstm_b_gate'][j],
                                       W['mlstm_norm'][j], W['mlstm_w_out'][j],
                                       seq=S, tl=tl, L=L, per_chunk_state=per_chunk_state)
            C_new.append(C)
            n_new.append(n[:, :, 0, :])
            m_new.append(m[:, :, 0, 0])
        x2 = _ffn(x2, W['norm_ffn2'][i], W['ffn2_w_in'][i], W['ffn2_w_out'][i], W['norm_final'],
                  final_norm=(i == DEPTH - 1), tm=tm)
    return (x2.reshape(B, S, D_MODEL), jnp.stack(pool_new), jnp.stack(C_new), jnp.stack(n_new),
            jnp.stack(m_new))


def kernel(x_prompt, x_sample, state_pool, state_mlstm_C, state_mlstm_n, state_mlstm_m, norm_ffn1,
           ffn1_w_in, ffn1_w_out, norm_mix, pool_w, pool_b, pool_scale, mlstm_w_in, mlstm_b_i, mlstm_b_f,
           mlstm_norm, mlstm_w_out, norm_ffn2, ffn2_w_in, ffn2_w_out, norm_final):
    n_gate = 2 * MLSTM_HEADS
    W = dict(
        norm_ffn1=norm_ffn1[:, None, :], norm_ffn2=norm_ffn2[:, None, :], norm_mix=norm_mix[:, None, :],
        norm_final=norm_final[None, :],
        ffn1_w_in=ffn1_w_in.astype(BF16), ffn1_w_out=ffn1_w_out.astype(BF16),
        ffn2_w_in=ffn2_w_in.astype(BF16), ffn2_w_out=ffn2_w_out.astype(BF16),
        pool_w=pool_w.astype(BF16), pool_b=pool_b[:, None, :], pool_scale=pool_scale[:, None, :],
        mlstm_w_main=mlstm_w_in[:, :, :MLSTM_MAIN].astype(BF16),
        mlstm_w_gate=jnp.pad(mlstm_w_in[:, :, MLSTM_MAIN:], ((0, 0), (0, 0), (0, GATE_LANES - n_gate))).astype(BF16),
        mlstm_b_gate=jnp.pad(jnp.concatenate([mlstm_b_i, mlstm_b_f], axis=-1),
                             ((0, 0), (0, GATE_LANES - n_gate)))[:, None, :],
        mlstm_norm=mlstm_norm[:, None, :], mlstm_w_out=mlstm_w_out.astype(BF16),
    )
    B = x_prompt.shape[0]
    zero_pool = jnp.zeros((state_pool.shape[0], B) + state_pool.shape[2:], state_pool.dtype)
    zero_C = jnp.zeros((state_mlstm_C.shape[0], B) + state_mlstm_C.shape[2:], state_mlstm_C.dtype)
    zero_n = jnp.zeros((state_mlstm_n.shape[0], B) + state_mlstm_n.shape[2:], state_mlstm_n.dtype)
    zero_m = jnp.zeros((state_mlstm_m.shape[0], B) + state_mlstm_m.shape[2:], state_mlstm_m.dtype)
    y_p, pool_p, C_p, n_p, m_p = _trunk(x_prompt, zero_pool, zero_C, zero_n, zero_m, 0, W,
                                        tm=512, ts=512, tl=512, L=256, per_chunk_state=False)
    Bs, Ss, _ = x_sample.shape
    y_s, pool_s, C_s, n_s, m_s = _trunk(x_sample, state_pool, state_mlstm_C, state_mlstm_n, state_mlstm_m,
                                        PAST_LEN, W, tm=Bs * Ss, ts=Ss, tl=Bs * Ss, L=Ss, per_chunk_state=True)
    return (y_p, y_s, pool_p, C_p, n_p, m_p, pool_s, C_s, n_s, m_s)
```

```python
import functools

import jax
import jax.numpy as jnp
from jax import lax
from jax.experimental import pallas as pl
from jax.experimental.pallas import tpu as pltpu

D_MODEL = 1024
D_FF = 2816
DEPTH = 4
PAST_LEN = 1024
POOL_WINDOWS = (2, 4, 8, 16)
POOL_GROUP_WIDTH = D_MODEL // len(POOL_WINDOWS)
POOL_HIST = max(POOL_WINDOWS) - 1
HIST_ROWS = POOL_HIST + 1
MLSTM_HEADS = 4
MLSTM_DQK = D_MODEL // (2 * MLSTM_HEADS)
MLSTM_DV = D_MODEL // MLSTM_HEADS
MLSTM_QK = MLSTM_HEADS * MLSTM_DQK
MLSTM_V = MLSTM_HEADS * MLSTM_DV
MLSTM_MAIN = 2 * MLSTM_QK + 2 * MLSTM_V
GATE_LANES = 128
EPS = 1e-6

VMEM_LIMIT_BYTES = 56 * 1024 * 1024

F32 = jnp.float32
BF16 = jnp.bfloat16


def _rmsnorm_f32(x, g):
    return x * lax.rsqrt(jnp.mean(x * x, axis=-1, keepdims=True) + EPS) * g


def _resident(shape, layer, col=0):
    idx = (layer,) + (0,) * (len(shape) - 1) + (col,)
    return pl.BlockSpec((None,) + tuple(shape), lambda *_: idx, pipeline_mode=pl.Buffered(1))


def _ffn_body(x_ref, g_ref, wa_ref, wb_ref, wo_ref, gf_ref, o_ref, *, final_norm):
    x = x_ref[...]
    xn = _rmsnorm_f32(x, g_ref[...]).astype(BF16)
    a = jnp.dot(xn, wa_ref[...], preferred_element_type=F32)
    b = jnp.dot(xn, wb_ref[...], preferred_element_type=F32)
    h = (a * jax.nn.sigmoid(a) * b).astype(BF16)
    y = x + 0.5 * jnp.dot(h, wo_ref[...], preferred_element_type=F32)
    if final_norm:
        y = _rmsnorm_f32(y, gf_ref[...])
    o_ref[...] = y


def _ffn(x, g, w_in, w_out, g_final, *, layer, final_norm, tm):
    T = x.shape[0]
    assert T % tm == 0
    row = pl.BlockSpec((tm, D_MODEL), lambda i: (i, 0))
    return pl.pallas_call(
        functools.partial(_ffn_body, final_norm=final_norm),
        grid=(T // tm,),
        in_specs=[
            row,
            _resident((1, D_MODEL), layer),
            _resident((D_MODEL, D_FF), layer, 0),
            _resident((D_MODEL, D_FF), layer, 1),
            _resident((D_FF, D_MODEL), layer),
            _resident((1, D_MODEL), 0),
        ],
        out_specs=row,
        out_shape=jax.ShapeDtypeStruct((T, D_MODEL), F32),
        compiler_params=pltpu.CompilerParams(
            dimension_semantics=("arbitrary",), vmem_limit_bytes=VMEM_LIMIT_BYTES),
        name="ffn",
    )(x, g, w_in, w_in, w_out, g_final)


def _pool_body(x_ref, hist_ref, g_ref, w_ref, b_ref, sc_ref, o_ref, hist_out_ref, ext_ref,
               *, ts, pos0, n_tiles):
    t = pl.program_id(1)
    x = x_ref[0]
    u = _rmsnorm_f32(x, g_ref[...])

    @pl.when(t == 0)
    def _():
        ext_ref[0:HIST_ROWS, :] = hist_ref[0]

    ext_ref[HIST_ROWS:HIST_ROWS + ts, :] = u
    ext = ext_ref[...]
    pos = pos0 + t * ts + lax.broadcasted_iota(jnp.int32, (ts, 1), 0)
    ys = []
    for gi, w in enumerate(POOL_WINDOWS):
        c0 = gi * POOL_GROUP_WIDTH
        s = ext[:, c0:c0 + POOL_GROUP_WIDTH]
        k = 1
        while k < w:
            s = s + pltpu.roll(s, k, 0)
            k *= 2
        cnt = jnp.minimum(pos + 1, w).astype(F32)
        d = s[HIST_ROWS:, :] / cnt - u[:, c0:c0 + POOL_GROUP_WIDTH]
        ys.append(jnp.dot(d.astype(BF16), w_ref[gi], preferred_element_type=F32))
    y = (jnp.concatenate(ys, axis=-1) + b_ref[...]) * sc_ref[...]
    o_ref[0] = x + y
    tail = ext_ref[ts:ts + HIST_ROWS, :]
    ext_ref[0:HIST_ROWS, :] = tail

    @pl.when(t == n_tiles - 1)
    def _():
        hist_out_ref[0] = tail


def _pool_mixer(x, hist, g, w, b, sc, *, layer, pool_layer, pos0, ts):
    B, S, _ = x.shape
    assert S % ts == 0 and ts >= HIST_ROWS
    n_tiles = S // ts
    return pl.pallas_call(
        functools.partial(_pool_body, ts=ts, pos0=pos0, n_tiles=n_tiles),
        grid=(B, n_tiles),
        in_specs=[
            pl.BlockSpec((1, ts, D_MODEL), lambda bi, t: (bi, t, 0)),
            pl.BlockSpec((None, 1, HIST_ROWS, D_MODEL), lambda bi, t: (pool_layer, bi, 0, 0)),
            _resident((1, D_MODEL), layer),
            _resident(w.shape[1:], pool_layer),
            _resident((1, D_MODEL), pool_layer),
            _resident((1, D_MODEL), pool_layer),
        ],
        out_specs=[
            pl.BlockSpec((1, ts, D_MODEL), lambda bi, t: (bi, t, 0)),
            pl.BlockSpec((1, HIST_ROWS, D_MODEL), lambda bi, t: (bi, 0, 0)),
        ],
        out_shape=[
            jax.ShapeDtypeStruct((B, S, D_MODEL), F32),
            jax.ShapeDtypeStruct((B, HIST_ROWS, D_MODEL), F32),
        ],
        scratch_shapes=[pltpu.VMEM((HIST_ROWS + ts, D_MODEL), F32)],
        compiler_params=pltpu.CompilerParams(
            dimension_semantics=("arbitrary", "arbitrary"), vmem_limit_bytes=VMEM_LIMIT_BYTES),
        name="pool_mixer",
    )(x, hist, g, w, b, sc)


def _log_sigmoid(x):
    return jnp.minimum(x, 0.0) - jnp.log1p(jnp.exp(-jnp.abs(x)))


def _dot_nt(a, b):
    return lax.dot_general(a, b, (((1,), (1,)), ((), ())), preferred_element_type=F32)


def _dot_tn(a, b):
    return lax.dot_general(a, b, (((0,), (0,)), ((), ())), preferred_element_type=F32)


def _mlstm_body(x_ref, c0_ref, n0_ref, m0_ref, g_ref, win_ref, wg_ref, bg_ref, gn_ref, wo_ref,
                o_ref, c_ref, n_ref, m_ref, p_s, gate_s, hn_s, *, tl, L, per_chunk_state):
    H = MLSTM_HEADS
    t = pl.program_id(1)

    @pl.when(t == 0)
    def _():
        c_ref[...] = c0_ref[...]
        n_ref[...] = n0_ref[...]
        m_ref[...] = m0_ref[...]

    x = x_ref[...]
    u = _rmsnorm_f32(x, g_ref[...]).astype(BF16)
    p_s[...] = jnp.dot(u, win_ref[...], preferred_element_type=F32)
    pre = jnp.dot(u, wg_ref[...], preferred_element_type=F32) + bg_ref[...]
    lane = lax.broadcasted_iota(jnp.int32, pre.shape, 1)
    gate_s[...] = jnp.where(lane < H, pre, _log_sigmoid(pre))

    row_g = lax.broadcasted_iota(jnp.int32, (L, GATE_LANES), 0)
    causal = lax.broadcasted_iota(jnp.int32, (L, L), 0) >= lax.broadcasted_iota(jnp.int32, (L, L), 1)
    for c in range(tl // L):
        r0 = c * L
        si = c if per_chunk_state else 0
        G = gate_s[r0:r0 + L, :]
        Bc = G
        k = 1
        while k < L:
            Bc = Bc + jnp.where(row_g >= k, pltpu.roll(Bc, k, 0), 0.0)
            k *= 2
        A = G - pltpu.roll(Bc, GATE_LANES - H, 1)
        AT = A.T
        for h in range(H):
            bcol = Bc[:, H + h:H + h + 1]
            acol = A[:, h:h + 1]
            arow = AT[h:h + 1, :]
            C = c_ref[si, h]
            n = n_ref[si, h]
            m_prev = m_ref[si, h][:, 0:1]
            q = p_s[r0:r0 + L, h * MLSTM_DQK:(h + 1) * MLSTM_DQK] * (MLSTM_DQK ** -0.5)
            kk = p_s[r0:r0 + L, MLSTM_QK + h * MLSTM_DQK:MLSTM_QK + (h + 1) * MLSTM_DQK]
            v0 = 2 * MLSTM_QK + h * MLSTM_DV
            v = p_s[r0:r0 + L, v0:v0 + MLSTM_DV]
            og = p_s[r0:r0 + L, v0 + MLSTM_V:v0 + MLSTM_V + MLSTM_DV]
            qb = q.astype(BF16)
            kb = kk.astype(BF16)

            log_d = jnp.where(causal, bcol + arow, -jnp.inf)
            log_state = bcol + m_prev
            m_t = jnp.maximum(jnp.max(log_d, axis=-1, keepdims=True), log_state)
            w_intra = jnp.exp(log_d - m_t)
            w_state = jnp.exp(log_state - m_t)
            s = _dot_nt(qb, kb) * w_intra
            num = (jnp.dot(s.astype(BF16), v.astype(BF16), preferred_element_type=F32)
                   + w_state * _dot_nt(qb, C.astype(BF16)))
            den = jnp.sum(s, axis=-1, keepdims=True) + w_state * jnp.sum(q * n, axis=-1, keepdims=True)
            hh = num / jnp.maximum(jnp.abs(den), jnp.exp(-m_t))
            hh = hh * lax.rsqrt(jnp.mean(hh * hh, axis=-1, keepdims=True) + EPS)
            hh = hh * gn_ref[:, h * MLSTM_DV:(h + 1) * MLSTM_DV]
            hn_s[r0:r0 + L, h * MLSTM_DV:(h + 1) * MLSTM_DV] = (jax.nn.sigmoid(og) * hh).astype(BF16)

            m_new = m_t[L - 1:L, :]
            b_last = bcol[L - 1:L, :]
            w_carry = jnp.exp(b_last + m_prev - m_new)
            w_write = jnp.exp(b_last + acol - m_new)
            c_ref[si, h] = w_carry * C + _dot_tn((v * w_write).astype(BF16), kb)
            n_ref[si, h] = w_carry * n + jnp.sum(kk * w_write, axis=0, keepdims=True)
            m_ref[si, h] = jnp.broadcast_to(m_new, (1, GATE_LANES))

    o_ref[...] = x + jnp.dot(hn_s[...], wo_ref[...], preferred_element_type=F32)


def _mlstm_mixer(x, C0, n0, m0, g, w_main, w_gate, b_gate, g_norm, w_out,
                 *, layer, cell_layer, seq, tl, L, per_chunk_state):
    T = x.shape[0]
    H = MLSTM_HEADS
    B = C0.shape[1]
    jl = cell_layer
    nb = B if per_chunk_state else 1
    tiles_per_seq = 1 if per_chunk_state else seq // tl
    if per_chunk_state:
        assert seq == L and tl == nb * L and T == tl
    else:
        assert seq % tl == 0 and tl % L == 0 and T == B * seq
    row = pl.BlockSpec((tl, D_MODEL), lambda b, t: (b * tiles_per_seq + t, 0))
    c_spec = pl.BlockSpec((nb, H, MLSTM_DV, MLSTM_DQK), lambda b, t: (b, 0, 0, 0))
    n_spec = pl.BlockSpec((nb, H, 1, MLSTM_DQK), lambda b, t: (b, 0, 0, 0))
    m_spec = pl.BlockSpec((nb, H, 1, GATE_LANES), lambda b, t: (b, 0, 0, 0))
    return pl.pallas_call(
        functools.partial(_mlstm_body, tl=tl, L=L, per_chunk_state=per_chunk_state),
        grid=(B // nb, tiles_per_seq),
        in_specs=[
            row,
            pl.BlockSpec((None, nb, H, MLSTM_DV, MLSTM_DQK), lambda b, t: (jl, b, 0, 0, 0)),
            pl.BlockSpec((None, nb, H, 1, MLSTM_DQK), lambda b, t: (jl, b, 0, 0, 0)),
            pl.BlockSpec((None, nb, H, 1, GATE_LANES), lambda b, t: (jl, b, 0, 0, 0)),
            _resident((1, D_MODEL), layer),
            _resident((D_MODEL, MLSTM_MAIN), jl),
            _resident((D_MODEL, GATE_LANES), jl),
            _resident((1, GATE_LANES), jl),
            _resident((1, MLSTM_V), jl),
            _resident((MLSTM_V, D_MODEL), jl),
        ],
        out_specs=[row, c_spec, n_spec, m_spec],
        out_shape=[
            jax.ShapeDtypeStruct((T, D_MODEL), F32),
            jax.ShapeDtypeStruct((B, H, MLSTM_DV, MLSTM_DQK), F32),
            jax.ShapeDtypeStruct((B, H, 1, MLSTM_DQK), F32),
            jax.ShapeDtypeStruct((B, H, 1, GATE_LANES), F32),
        ],
        scratch_shapes=[
            pltpu.VMEM((tl, MLSTM_MAIN), F32),
            pltpu.VMEM((tl, GATE_LANES), F32),
            pltpu.VMEM((tl, MLSTM_V), BF16),
        ],
        compiler_params=pltpu.CompilerParams(
            dimension_semantics=("arbitrary", "arbitrary"), vmem_limit_bytes=VMEM_LIMIT_BYTES),
        name="mlstm_layer",
    )(x, C0, n0, m0, g, w_main, w_gate, b_gate, g_norm, w_out)


def _trunk(x, pool_hist, mC, mn, mm, pos0, W, *, tm, ts, tl, L, per_chunk_state):
    B, S, _ = x.shape
    pool_new, C_new, n_new, m_new = [], [], [], []
    hist = jnp.pad(pool_hist, ((0, 0), (0, 0), (1, 0), (0, 0)))
    n0 = mn[:, :, :, None, :]
    m0 = jnp.broadcast_to(mm[:, :, :, None, None], mm.shape + (1, GATE_LANES))
    x2 = x.reshape(B * S, D_MODEL)
    for i in range(DEPTH):
        j = i // 2
        x2 = _ffn(x2, W['norm_ffn1'], W['ffn1_w_in'], W['ffn1_w_out'], W['norm_final'],
                  layer=i, final_norm=False, tm=tm)
        if i % 2 == 0:
            x3, hist_new = _pool_mixer(x2.reshape(B, S, D_MODEL), hist, W['norm_mix'], W['pool_w'],
                                       W['pool_b'], W['pool_scale'], layer=i, pool_layer=j, pos0=pos0, ts=ts)
            x2 = x3.reshape(B * S, D_MODEL)
            pool_new.append(hist_new[:, 1:])
        else:
            x2, C, n, m = _mlstm_mixer(x2, mC, n0, m0, W['norm_mix'], W['mlstm_w_in'], W['mlstm_w_gate'],
                                       W['mlstm_b_gate'], W['mlstm_norm'], W['mlstm_w_out'],
                                       layer=i, cell_layer=j, seq=S, tl=tl, L=L, per_chunk_state=per_chunk_state)
            C_new.append(C)
            n_new.append(n[:, :, 0, :])
            m_new.append(m[:, :, 0, 0])
        x2 = _ffn(x2, W['norm_ffn2'], W['ffn2_w_in'], W['ffn2_w_out'], W['norm_final'],
                  layer=i, final_norm=(i == DEPTH - 1), tm=tm)
    return (x2.reshape(B, S, D_MODEL), jnp.stack(pool_new), jnp.stack(C_new), jnp.stack(n_new),
            jnp.stack(m_new))


def kernel(x_prompt, x_sample, state_pool, state_mlstm_C, state_mlstm_n, state_mlstm_m, norm_ffn1,
           ffn1_w_in, ffn1_w_out, norm_mix, pool_w, pool_b, pool_scale, mlstm_w_in, mlstm_b_i, mlstm_b_f,
           mlstm_norm, mlstm_w_out, norm_ffn2, ffn2_w_in, ffn2_w_out, norm_final):
    n_gate = 2 * MLSTM_HEADS
    W = dict(
        norm_ffn1=norm_ffn1[:, None, :], norm_ffn2=norm_ffn2[:, None, :], norm_mix=norm_mix[:, None, :],
        norm_final=norm_final[None, None, :],
        ffn1_w_in=ffn1_w_in.astype(BF16), ffn1_w_out=ffn1_w_out.astype(BF16),
        ffn2_w_in=ffn2_w_in.astype(BF16), ffn2_w_out=ffn2_w_out.astype(BF16),
        pool_w=pool_w.astype(BF16), pool_b=pool_b[:, None, :], pool_scale=pool_scale[:, None, :],
        mlstm_w_in=mlstm_w_in.astype(BF16),
        mlstm_w_gate=jnp.pad(mlstm_w_in[:, :, MLSTM_MAIN:], ((0, 0), (0, 0), (0, GATE_LANES - n_gate))).astype(BF16),
        mlstm_b_gate=jnp.pad(jnp.concatenate([mlstm_b_i, mlstm_b_f], axis=-1),
                             ((0, 0), (0, GATE_LANES - n_gate)))[:, None, :],
        mlstm_norm=mlstm_norm[:, None, :], mlstm_w_out=mlstm_w_out.astype(BF16),
    )
    B = x_prompt.shape[0]
    zero_pool = jnp.zeros((state_pool.shape[0], B) + state_pool.shape[2:], state_pool.dtype)
    zero_C = jnp.zeros((state_mlstm_C.shape[0], B) + state_mlstm_C.shape[2:], state_mlstm_C.dtype)
    zero_n = jnp.zeros((state_mlstm_n.shape[0], B) + state_mlstm_n.shape[2:], state_mlstm_n.dtype)
    zero_m = jnp.zeros((state_mlstm_m.shape[0], B) + state_mlstm_m.shape[2:], state_mlstm_m.dtype)
    y_p, pool_p, C_p, n_p, m_p = _trunk(x_prompt, zero_pool, zero_C, zero_n, zero_m, 0, W,
                                        tm=512, ts=512, tl=512, L=256, per_chunk_state=False)
    Bs, Ss, _ = x_sample.shape
    y_s, pool_s, C_s, n_s, m_s = _trunk(x_sample, state_pool, state_mlstm_C, state_mlstm_n, state_mlstm_m,
                                        PAST_LEN, W, tm=Bs * Ss, ts=Ss, tl=Bs * Ss, L=Ss, per_chunk_state=True)
    return (y_p, y_s, pool_p, C_p, n_p, m_p, pool_s, C_s, n_s, m_s)
```

```python
import functools

import jax
import jax.numpy as jnp
from jax import lax
from jax.experimental import pallas as pl
from jax.experimental.pallas import tpu as pltpu

D_MODEL = 1024
D_FF = 2816
DEPTH = 4
PAST_LEN = 1024
POOL_WINDOWS = (2, 4, 8, 16)
POOL_GROUP_WIDTH = D_MODEL // len(POOL_WINDOWS)
POOL_HIST = max(POOL_WINDOWS) - 1
HIST_ROWS = POOL_HIST + 1
MLSTM_HEADS = 4
MLSTM_DQK = D_MODEL // (2 * MLSTM_HEADS)
MLSTM_DV = D_MODEL // MLSTM_HEADS
MLSTM_QK = MLSTM_HEADS * MLSTM_DQK
MLSTM_V = MLSTM_HEADS * MLSTM_DV
MLSTM_MAIN = 2 * MLSTM_QK + 2 * MLSTM_V
GATE_LANES = 128
EPS = 1e-6

VMEM_LIMIT_BYTES = 56 * 1024 * 1024
SUB_ROWS = 512

F32 = jnp.float32
BF16 = jnp.bfloat16


def _rmsnorm_f32(x, g):
    return x * lax.rsqrt(jnp.mean(x * x, axis=-1, keepdims=True) + EPS) * g


def _resident(shape, layer, col=0):
    idx = (layer,) + (0,) * (len(shape) - 1) + (col,)
    return pl.BlockSpec((None,) + tuple(shape), lambda *_: idx, pipeline_mode=pl.Buffered(1))


def _ffn_body(x_ref, g_ref, wa_ref, wb_ref, wo_ref, gf_ref, o_ref, *, final_norm, sub):
    for r0 in range(0, x_ref.shape[0], sub):
        x = x_ref[r0:r0 + sub, :]
        xn = _rmsnorm_f32(x, g_ref[...]).astype(BF16)
        a = jnp.dot(xn, wa_ref[...], preferred_element_type=F32)
        b = jnp.dot(xn, wb_ref[...], preferred_element_type=F32)
        h = (a * jax.nn.sigmoid(a) * b).astype(BF16)
        y = x + 0.5 * jnp.dot(h, wo_ref[...], preferred_element_type=F32)
        if final_norm:
            y = _rmsnorm_f32(y, gf_ref[...])
        o_ref[r0:r0 + sub, :] = y


def _ffn(x, g, w_in, w_out, g_final, *, layer, final_norm, tm):
    T = x.shape[0]
    assert T % tm == 0
    row = pl.BlockSpec((tm, D_MODEL), lambda i: (i, 0))
    return pl.pallas_call(
        functools.partial(_ffn_body, final_norm=final_norm, sub=min(tm, SUB_ROWS)),
        grid=(T // tm,),
        in_specs=[
            row,
            _resident((1, D_MODEL), layer),
            _resident((D_MODEL, D_FF), layer, 0),
            _resident((D_MODEL, D_FF), layer, 1),
            _resident((D_FF, D_MODEL), layer),
            _resident((1, D_MODEL), 0),
        ],
        out_specs=row,
        out_shape=jax.ShapeDtypeStruct((T, D_MODEL), F32),
        compiler_params=pltpu.CompilerParams(
            dimension_semantics=("arbitrary",), vmem_limit_bytes=VMEM_LIMIT_BYTES),
        name="ffn",
    )(x, g, w_in, w_in, w_out, g_final)


def _pool_body(x_ref, hist_ref, g_ref, w_ref, b_ref, sc_ref, o_ref, hist_out_ref, ext_ref,
               *, ts, pos0, n_tiles):
    t = pl.program_id(1)
    x = x_ref[0]
    u = _rmsnorm_f32(x, g_ref[...])

    @pl.when(t == 0)
    def _():
        ext_ref[0:HIST_ROWS, :] = hist_ref[0]

    ext_ref[HIST_ROWS:HIST_ROWS + ts, :] = u
    ext = ext_ref[...]
    pos = pos0 + t * ts + lax.broadcasted_iota(jnp.int32, (ts, 1), 0)
    ys = []
    for gi, w in enumerate(POOL_WINDOWS):
        c0 = gi * POOL_GROUP_WIDTH
        s = ext[:, c0:c0 + POOL_GROUP_WIDTH]
        k = 1
        while k < w:
            s = s + pltpu.roll(s, k, 0)
            k *= 2
        cnt = jnp.minimum(pos + 1, w).astype(F32)
        d = s[HIST_ROWS:, :] / cnt - u[:, c0:c0 + POOL_GROUP_WIDTH]
        ys.append(jnp.dot(d.astype(BF16), w_ref[gi], preferred_element_type=F32))
    y = (jnp.concatenate(ys, axis=-1) + b_ref[...]) * sc_ref[...]
    o_ref[0] = x + y
    tail = ext_ref[ts:ts + HIST_ROWS, :]
    ext_ref[0:HIST_ROWS, :] = tail

    @pl.when(t == n_tiles - 1)
    def _():
        hist_out_ref[0] = tail


def _pool_mixer(x, hist, g, w, b, sc, *, layer, pool_layer, pos0, ts):
    B, S, _ = x.shape
    assert S % ts == 0 and ts >= HIST_ROWS
    n_tiles = S // ts
    return pl.pallas_call(
        functools.partial(_pool_body, ts=ts, pos0=pos0, n_tiles=n_tiles),
        grid=(B, n_tiles),
        in_specs=[
            pl.BlockSpec((1, ts, D_MODEL), lambda bi, t: (bi, t, 0)),
            pl.BlockSpec((None, 1, HIST_ROWS, D_MODEL), lambda bi, t: (pool_layer, bi, 0, 0)),
            _resident((1, D_MODEL), layer),
            _resident(w.shape[1:], pool_layer),
            _resident((1, D_MODEL), pool_layer),
            _resident((1, D_MODEL), pool_layer),
        ],
        out_specs=[
            pl.BlockSpec((1, ts, D_MODEL), lambda bi, t: (bi, t, 0)),
            pl.BlockSpec((1, HIST_ROWS, D_MODEL), lambda bi, t: (bi, 0, 0)),
        ],
        out_shape=[
            jax.ShapeDtypeStruct((B, S, D_MODEL), F32),
            jax.ShapeDtypeStruct((B, HIST_ROWS, D_MODEL), F32),
        ],
        scratch_shapes=[pltpu.VMEM((HIST_ROWS + ts, D_MODEL), F32)],
        compiler_params=pltpu.CompilerParams(
            dimension_semantics=("arbitrary", "arbitrary"), vmem_limit_bytes=VMEM_LIMIT_BYTES),
        name="pool_mixer",
    )(x, hist, g, w, b, sc)


def _log_sigmoid(x):
    return jnp.minimum(x, 0.0) - jnp.log1p(jnp.exp(-jnp.abs(x)))


def _dot_nt(a, b):
    return lax.dot_general(a, b, (((1,), (1,)), ((), ())), preferred_element_type=F32)


def _dot_tn(a, b):
    return lax.dot_general(a, b, (((0,), (0,)), ((), ())), preferred_element_type=F32)


def _mlstm_body(x_ref, c0_ref, n0_ref, m0_ref, g_ref, win_ref, wg_ref, bg_ref, gn_ref, wo_ref,
                o_ref, c_ref, n_ref, m_ref, *scratch, tl, L, per_chunk_state, sub):
    H = MLSTM_HEADS
    t = pl.program_id(1)
    n_sub = tl // sub
    p_ss, gate_ss, hn_ss = scratch[:n_sub], scratch[n_sub:2 * n_sub], scratch[2 * n_sub:]

    @pl.when(t == 0)
    def _():
        c_ref[...] = c0_ref[...]
        n_ref[...] = n0_ref[...]
        m_ref[...] = m0_ref[...]

    for i_sub in range(n_sub):
        x = x_ref[i_sub * sub:(i_sub + 1) * sub, :]
        u = _rmsnorm_f32(x, g_ref[...]).astype(BF16)
        p_ss[i_sub][...] = jnp.dot(u, win_ref[...], preferred_element_type=F32)
        pre = jnp.dot(u, wg_ref[...], preferred_element_type=F32) + bg_ref[...]
        lane = lax.broadcasted_iota(jnp.int32, pre.shape, 1)
        gate_ss[i_sub][...] = jnp.where(lane < H, pre, _log_sigmoid(pre))

    row_g = lax.broadcasted_iota(jnp.int32, (L, GATE_LANES), 0)
    causal = lax.broadcasted_iota(jnp.int32, (L, L), 0) >= lax.broadcasted_iota(jnp.int32, (L, L), 1)
    for c in range(tl // L):
        i_sub, r0 = divmod(c * L, sub)
        p_s, gate_s, hn_s = p_ss[i_sub], gate_ss[i_sub], hn_ss[i_sub]
        si = c if per_chunk_state else 0
        G = gate_s[r0:r0 + L, :]
        Bc = G
        k = 1
        while k < L:
            Bc = Bc + jnp.where(row_g >= k, pltpu.roll(Bc, k, 0), 0.0)
            k *= 2
        A = G - pltpu.roll(Bc, GATE_LANES - H, 1)
        AT = A.T
        for h in range(H):
            bcol = Bc[:, H + h:H + h + 1]
            acol = A[:, h:h + 1]
            arow = AT[h:h + 1, :]
            C = c_ref[si, h]
            n = n_ref[si, h]
            m_prev = m_ref[si, h][:, 0:1]
            q = p_s[r0:r0 + L, h * MLSTM_DQK:(h + 1) * MLSTM_DQK] * (MLSTM_DQK ** -0.5)
            kk = p_s[r0:r0 + L, MLSTM_QK + h * MLSTM_DQK:MLSTM_QK + (h + 1) * MLSTM_DQK]
            v0 = 2 * MLSTM_QK + h * MLSTM_DV
            v = p_s[r0:r0 + L, v0:v0 + MLSTM_DV]
            og = p_s[r0:r0 + L, v0 + MLSTM_V:v0 + MLSTM_V + MLSTM_DV]
            qb = q.astype(BF16)
            kb = kk.astype(BF16)

            log_d = jnp.where(causal, bcol + arow, -jnp.inf)
            log_state = bcol + m_prev
            m_t = jnp.maximum(jnp.max(log_d, axis=-1, keepdims=True), log_state)
            w_intra = jnp.exp(log_d - m_t)
            w_state = jnp.exp(log_state - m_t)
            s = _dot_nt(qb, kb) * w_intra
            num = (jnp.dot(s.astype(BF16), v.astype(BF16), preferred_element_type=F32)
                   + w_state * _dot_nt(qb, C.astype(BF16)))
            den = jnp.sum(s, axis=-1, keepdims=True) + w_state * jnp.sum(q * n, axis=-1, keepdims=True)
            hh = num / jnp.maximum(jnp.abs(den), jnp.exp(-m_t))
            hh = hh * lax.rsqrt(jnp.mean(hh * hh, axis=-1, keepdims=True) + EPS)
            hh = hh * gn_ref[:, h * MLSTM_DV:(h + 1) * MLSTM_DV]
            hn_s[r0:r0 + L, h * MLSTM_DV:(h + 1) * MLSTM_DV] = (jax.nn.sigmoid(og) * hh).astype(BF16)

            m_new = m_t[L - 1:L, :]
            b_last = bcol[L - 1:L, :]
            w_carry = jnp.exp(b_last + m_prev - m_new)
            w_write = jnp.exp(b_last + acol - m_new)
            c_ref[si, h] = w_carry * C + _dot_tn((v * w_write).astype(BF16), kb)
            n_ref[si, h] = w_carry * n + jnp.sum(kk * w_write, axis=0, keepdims=True)
            m_ref[si, h] = jnp.broadcast_to(m_new, (1, GATE_LANES))

        if r0 + L == sub:
            rows = slice(i_sub * sub, (i_sub + 1) * sub)
            o_ref[rows, :] = x_ref[rows, :] + jnp.dot(hn_s[...], wo_ref[...], preferred_element_type=F32)


def _mlstm_mixer(x, C0, n0, m0, g, w_main, w_gate, b_gate, g_norm, w_out,
                 *, layer, cell_layer, seq, tl, L, per_chunk_state):
    T = x.shape[0]
    H = MLSTM_HEADS
    B = C0.shape[1]
    jl = cell_layer
    sub = min(tl, SUB_ROWS)
    n_sub = tl // sub
    nb = B if per_chunk_state else 1
    tiles_per_seq = 1 if per_chunk_state else seq // tl
    if per_chunk_state:
        assert seq == L and tl == nb * L and T == tl
    else:
        assert seq % tl == 0 and T == B * seq
    assert tl % sub == 0 and sub % L == 0
    row = pl.BlockSpec((tl, D_MODEL), lambda b, t: (b * tiles_per_seq + t, 0))
    c_spec = pl.BlockSpec((nb, H, MLSTM_DV, MLSTM_DQK), lambda b, t: (b, 0, 0, 0))
    n_spec = pl.BlockSpec((nb, H, 1, MLSTM_DQK), lambda b, t: (b, 0, 0, 0))
    m_spec = pl.BlockSpec((nb, H, 1, GATE_LANES), lambda b, t: (b, 0, 0, 0))
    return pl.pallas_call(
        functools.partial(_mlstm_body, tl=tl, L=L, per_chunk_state=per_chunk_state, sub=sub),
        grid=(B // nb, tiles_per_seq),
        in_specs=[
            row,
            pl.BlockSpec((None, nb, H, MLSTM_DV, MLSTM_DQK), lambda b, t: (jl, b, 0, 0, 0)),
            pl.BlockSpec((None, nb, H, 1, MLSTM_DQK), lambda b, t: (jl, b, 0, 0, 0)),
            pl.BlockSpec((None, nb, H, 1, GATE_LANES), lambda b, t: (jl, b, 0, 0, 0)),
            _resident((1, D_MODEL), layer),
            _resident((D_MODEL, MLSTM_MAIN), jl),
            _resident((D_MODEL, GATE_LANES), jl),
            _resident((1, GATE_LANES), jl),
            _resident((1, MLSTM_V), jl),
            _resident((MLSTM_V, D_MODEL), jl),
        ],
        out_specs=[row, c_spec, n_spec, m_spec],
        out_shape=[
            jax.ShapeDtypeStruct((T, D_MODEL), F32),
            jax.ShapeDtypeStruct((B, H, MLSTM_DV, MLSTM_DQK), F32),
            jax.ShapeDtypeStruct((B, H, 1, MLSTM_DQK), F32),
            jax.ShapeDtypeStruct((B, H, 1, GATE_LANES), F32),
        ],
        scratch_shapes=([pltpu.VMEM((sub, MLSTM_MAIN), F32)] * n_sub
                        + [pltpu.VMEM((sub, GATE_LANES), F32)] * n_sub
                        + [pltpu.VMEM((sub, MLSTM_V), BF16)] * n_sub),
        compiler_params=pltpu.CompilerParams(
            dimension_semantics=("arbitrary", "arbitrary"), vmem_limit_bytes=VMEM_LIMIT_BYTES),
        name="mlstm_layer",
    )(x, C0, n0, m0, g, w_main, w_gate, b_gate, g_norm, w_out)


def _trunk(x, pool_hist, mC, mn, mm, pos0, W, *, tm, ts, tl, L, per_chunk_state):
    B, S, _ = x.shape
    pool_new, C_new, n_new, m_new = [], [], [], []
    hist = jnp.pad(pool_hist, ((0, 0), (0, 0), (1, 0), (0, 0)))
    n0 = mn[:, :, :, None, :]
    m0 = jnp.broadcast_to(mm[:, :, :, None, None], mm.shape + (1, GATE_LANES))
    x2 = x.reshape(B * S, D_MODEL)
    for i in range(DEPTH):
        j = i // 2
        x2 = _ffn(x2, W['norm_ffn1'], W['ffn1_w_in'], W['ffn1_w_out'], W['norm_final'],
                  layer=i, final_norm=False, tm=tm)
        if i % 2 == 0:
            x3, hist_new = _pool_mixer(x2.reshape(B, S, D_MODEL), hist, W['norm_mix'], W['pool_w'],
                                       W['pool_b'], W['pool_scale'], layer=i, pool_layer=j, pos0=pos0, ts=ts)
            x2 = x3.reshape(B * S, D_MODEL)
            pool_new.append(hist_new[:, 1:])
        else:
            x2, C, n, m = _mlstm_mixer(x2, mC, n0, m0, W['norm_mix'], W['mlstm_w_in'], W['mlstm_w_gate'],
                                       W['mlstm_b_gate'], W['mlstm_norm'], W['mlstm_w_out'],
                                       layer=i, cell_layer=j, seq=S, tl=tl, L=L, per_chunk_state=per_chunk_state)
            C_new.append(C)
            n_new.append(n[:, :, 0, :])
            m_new.append(m[:, :, 0, 0])
        x2 = _ffn(x2, W['norm_ffn2'], W['ffn2_w_in'], W['ffn2_w_out'], W['norm_final'],
                  layer=i, final_norm=(i == DEPTH - 1), tm=tm)
    return (x2.reshape(B, S, D_MODEL), jnp.stack(pool_new), jnp.stack(C_new), jnp.stack(n_new),
            jnp.stack(m_new))


def kernel(x_prompt, x_sample, state_pool, state_mlstm_C, state_mlstm_n, state_mlstm_m, norm_ffn1,
           ffn1_w_in, ffn1_w_out, norm_mix, pool_w, pool_b, pool_scale, mlstm_w_in, mlstm_b_i, mlstm_b_f,
           mlstm_norm, mlstm_w_out, norm_ffn2, ffn2_w_in, ffn2_w_out, norm_final):
    n_gate = 2 * MLSTM_HEADS
    W = dict(
        norm_ffn1=norm_ffn1[:, None, :], norm_ffn2=norm_ffn2[:, None, :], norm_mix=norm_mix[:, None, :],
        norm_final=norm_final[None, None, :],
        ffn1_w_in=ffn1_w_in.astype(BF16), ffn1_w_out=ffn1_w_out.astype(BF16),
        ffn2_w_in=ffn2_w_in.astype(BF16), ffn2_w_out=ffn2_w_out.astype(BF16),
        pool_w=pool_w.astype(BF16), pool_b=pool_b[:, None, :], pool_scale=pool_scale[:, None, :],
        mlstm_w_in=mlstm_w_in.astype(BF16),
        mlstm_w_gate=jnp.pad(mlstm_w_in[:, :, MLSTM_MAIN:], ((0, 0), (0, 0), (0, GATE_LANES - n_gate))).astype(BF16),
        mlstm_b_gate=jnp.pad(jnp.concatenate([mlstm_b_i, mlstm_b_f], axis=-1),
                             ((0, 0), (0, GATE_LANES - n_gate)))[:, None, :],
        mlstm_norm=mlstm_norm[:, None, :], mlstm_w_out=mlstm_w_out.astype(BF16),
    )
    B = x_prompt.shape[0]
    zero_pool = jnp.zeros((state_pool.shape[0], B) + state_pool.shape[2:], state_pool.dtype)
    zero_C = jnp.zeros((state_mlstm_C.shape[0], B) + state_mlstm_C.shape[2:], state_mlstm_C.dtype)
    zero_n = jnp.zeros((state_mlstm_n.shape[0], B) + state_mlstm_n.shape[2:], state_mlstm_n.dtype)
    zero_m = jnp.zeros((state_mlstm_m.shape[0], B) + state_mlstm_m.shape[2:], state_mlstm_m.dtype)
    y_p, pool_p, C_p, n_p, m_p = _trunk(x_prompt, zero_pool, zero_C, zero_n, zero_m, 0, W,
                                        tm=1024, ts=512, tl=1024, L=256, per_chunk_state=False)
    Bs, Ss, _ = x_sample.shape
    y_s, pool_s, C_s, n_s, m_s = _trunk(x_sample, state_pool, state_mlstm_C, state_mlstm_n, state_mlstm_m,
                                        PAST_LEN, W, tm=Bs * Ss, ts=Ss, tl=Bs * Ss, L=Ss, per_chunk_state=True)
    return (y_p, y_s, pool_p, C_p, n_p, m_p, pool_s, C_s, n_s, m_s)
```

```python
import functools

import jax
import jax.numpy as jnp
from jax import lax
from jax.experimental import pallas as pl
from jax.experimental.pallas import tpu as pltpu

D_MODEL = 1024
D_FF = 2816
DEPTH = 4
PAST_LEN = 1024
POOL_WINDOWS = (2, 4, 8, 16)
POOL_GROUP_WIDTH = D_MODEL // len(POOL_WINDOWS)
POOL_HIST = max(POOL_WINDOWS) - 1
HIST_ROWS = POOL_HIST + 1
MLSTM_HEADS = 4
MLSTM_DQK = D_MODEL // (2 * MLSTM_HEADS)
MLSTM_DV = D_MODEL // MLSTM_HEADS
MLSTM_QK = MLSTM_HEADS * MLSTM_DQK
MLSTM_V = MLSTM_HEADS * MLSTM_DV
MLSTM_MAIN = 2 * MLSTM_QK + 2 * MLSTM_V
GATE_LANES = 128
EPS = 1e-6

VMEM_LIMIT_BYTES = 56 * 1024 * 1024
SUB_ROWS = 512

F32 = jnp.float32
BF16 = jnp.bfloat16


def _rmsnorm_f32(x, g):
    return x * lax.rsqrt(jnp.mean(x * x, axis=-1, keepdims=True) + EPS) * g


def _resident(shape, layer, col=0):
    idx = (layer,) + (0,) * (len(shape) - 1) + (col,)
    return pl.BlockSpec((None,) + tuple(shape), lambda *_: idx, pipeline_mode=pl.Buffered(1))


def _convert_slab(src_ref, dst_ref, in_buf, out_buf, in_sem, out_sem, *, step, n_steps):
    rows = in_buf.shape[1]
    slot = step % 2

    def in_copy(k, s):
        return pltpu.make_async_copy(src_ref.at[pl.ds(k * rows, rows), :], in_buf.at[s], in_sem.at[s])

    def out_copy(k, s):
        return pltpu.make_async_copy(out_buf.at[s], dst_ref.at[pl.ds(k * rows, rows), :], out_sem.at[s])

    @pl.when(step == 0)
    def _():
        in_copy(0, 0).start()

    @pl.when(step + 1 < n_steps)
    def _():
        in_copy(step + 1, 1 - slot).start()

    in_copy(step, slot).wait()

    @pl.when(step >= 2)
    def _():
        out_copy(step - 2, slot).wait()

    out_buf[slot] = in_buf[slot].astype(BF16)
    out_copy(step, slot).start()

    @pl.when(step == n_steps - 1)
    def _():
        if n_steps >= 2:
            out_copy(step - 1, 1 - slot).wait()
        out_copy(step, slot).wait()


def _ffn_body(x_ref, g_ref, wa_ref, wb_ref, wo_ref, gf_ref, *rest, final_norm, sub, n_steps, convert_layer):
    if convert_layer is not None:
        (nwi_ref, nwo_ref, o_ref, cwi_ref, cwo_ref,
         wi_in, wi_out, wo_in, wo_out, sem_ii, sem_io, sem_oi, sem_oo) = rest
        step = pl.program_id(0)
        _convert_slab(nwi_ref.at[convert_layer], cwi_ref, wi_in, wi_out, sem_ii, sem_io, step=step, n_steps=n_steps)
        _convert_slab(nwo_ref.at[convert_layer], cwo_ref, wo_in, wo_out, sem_oi, sem_oo, step=step, n_steps=n_steps)
    else:
        (o_ref,) = rest
    for r0 in range(0, x_ref.shape[0], sub):
        x = x_ref[r0:r0 + sub, :]
        xn = _rmsnorm_f32(x, g_ref[...]).astype(BF16)
        a = jnp.dot(xn, wa_ref[...], preferred_element_type=F32)
        b = jnp.dot(xn, wb_ref[...], preferred_element_type=F32)
        h = (a * jax.nn.sigmoid(a) * b).astype(BF16)
        y = x + 0.5 * jnp.dot(h, wo_ref[...], preferred_element_type=F32)
        if final_norm:
            y = _rmsnorm_f32(y, gf_ref[...])
        o_ref[r0:r0 + sub, :] = y


def _ffn(x, g, w_in, w_out, g_final, *, layer, final_norm, tm, convert=None):
    T = x.shape[0]
    assert T % tm == 0
    n_steps = T // tm
    row = pl.BlockSpec((tm, D_MODEL), lambda i: (i, 0))
    in_specs = [
        row,
        _resident((1, D_MODEL), layer),
        pl.BlockSpec((D_MODEL, D_FF), lambda i: (0, 0), pipeline_mode=pl.Buffered(1)),
        pl.BlockSpec((D_MODEL, D_FF), lambda i: (0, 1), pipeline_mode=pl.Buffered(1)),
        pl.BlockSpec((D_FF, D_MODEL), lambda i: (0, 0), pipeline_mode=pl.Buffered(1)),
        _resident((1, D_MODEL), 0),
    ]
    out_specs = [row]
    out_shape = [jax.ShapeDtypeStruct((T, D_MODEL), F32)]
    operands = [x, g, w_in, w_in, w_out, g_final]
    scratch = []
    convert_layer = None
    if convert is not None:
        src_in, src_out, convert_layer = convert
        assert D_MODEL % n_steps == 0 and D_FF % n_steps == 0
        r_in, r_out = D_MODEL // n_steps, D_FF // n_steps
        any_spec = pl.BlockSpec(memory_space=pl.ANY)
        in_specs += [any_spec, any_spec]
        out_specs += [any_spec, any_spec]
        out_shape += [jax.ShapeDtypeStruct((D_MODEL, 2 * D_FF), BF16), jax.ShapeDtypeStruct((D_FF, D_MODEL), BF16)]
        operands += [src_in, src_out]
        scratch = [
            pltpu.VMEM((2, r_in, 2 * D_FF), F32), pltpu.VMEM((2, r_in, 2 * D_FF), BF16),
            pltpu.VMEM((2, r_out, D_MODEL), F32), pltpu.VMEM((2, r_out, D_MODEL), BF16),
        ] + [pltpu.SemaphoreType.DMA((2,))] * 4
    outs = pl.pallas_call(
        functools.partial(_ffn_body, final_norm=final_norm, sub=min(tm, SUB_ROWS), n_steps=n_steps,
                          convert_layer=convert_layer),
        grid=(n_steps,),
        in_specs=in_specs,
        out_specs=out_specs,
        out_shape=out_shape,
        scratch_shapes=scratch,
        compiler_params=pltpu.CompilerParams(
            dimension_semantics=("arbitrary",), vmem_limit_bytes=VMEM_LIMIT_BYTES),
        name="ffn",
    )(*operands)
    return outs if convert is not None else outs[0]


def _pool_body(x_ref, hist_ref, g_ref, w_ref, b_ref, sc_ref, o_ref, hist_out_ref, ext_ref,
               *, ts, pos0, n_tiles):
    t = pl.program_id(1)
    x = x_ref[0]
    u = _rmsnorm_f32(x, g_ref[...])

    @pl.when(t == 0)
    def _():
        ext_ref[0:HIST_ROWS, :] = hist_ref[0]

    ext_ref[HIST_ROWS:HIST_ROWS + ts, :] = u
    ext = ext_ref[...]
    pos = pos0 + t * ts + lax.broadcasted_iota(jnp.int32, (ts, 1), 0)
    ys = []
    for gi, w in enumerate(POOL_WINDOWS):
        c0 = gi * POOL_GROUP_WIDTH
        s = ext[:, c0:c0 + POOL_GROUP_WIDTH]
        k = 1
        while k < w:
            s = s + pltpu.roll(s, k, 0)
            k *= 2
        cnt = jnp.minimum(pos + 1, w).astype(F32)
        d = s[HIST_ROWS:, :] / cnt - u[:, c0:c0 + POOL_GROUP_WIDTH]
        ys.append(jnp.dot(d.astype(BF16), w_ref[gi], preferred_element_type=F32))
    y = (jnp.concatenate(ys, axis=-1) + b_ref[...]) * sc_ref[...]
    o_ref[0] = x + y
    tail = ext_ref[ts:ts + HIST_ROWS, :]
    ext_ref[0:HIST_ROWS, :] = tail

    @pl.when(t == n_tiles - 1)
    def _():
        hist_out_ref[0] = tail


def _pool_mixer(x, hist, g, w, b, sc, *, layer, pool_layer, pos0, ts):
    B, S, _ = x.shape
    assert S % ts == 0 and ts >= HIST_ROWS
    n_tiles = S // ts
    return pl.pallas_call(
        functools.partial(_pool_body, ts=ts, pos0=pos0, n_tiles=n_tiles),
        grid=(B, n_tiles),
        in_specs=[
            pl.BlockSpec((1, ts, D_MODEL), lambda bi, t: (bi, t, 0)),
            pl.BlockSpec((None, 1, HIST_ROWS, D_MODEL), lambda bi, t: (pool_layer, bi, 0, 0)),
            _resident((1, D_MODEL), layer),
            _resident(w.shape[1:], pool_layer),
            _resident((1, D_MODEL), pool_layer),
            _resident((1, D_MODEL), pool_layer),
        ],
        out_specs=[
            pl.BlockSpec((1, ts, D_MODEL), lambda bi, t: (bi, t, 0)),
            pl.BlockSpec((1, HIST_ROWS, D_MODEL), lambda bi, t: (bi, 0, 0)),
        ],
        out_shape=[
            jax.ShapeDtypeStruct((B, S, D_MODEL), F32),
            jax.ShapeDtypeStruct((B, HIST_ROWS, D_MODEL), F32),
        ],
        scratch_shapes=[pltpu.VMEM((HIST_ROWS + ts, D_MODEL), F32)],
        compiler_params=pltpu.CompilerParams(
            dimension_semantics=("arbitrary", "arbitrary"), vmem_limit_bytes=VMEM_LIMIT_BYTES),
        name="pool_mixer",
    )(x, hist, g, w, b, sc)


def _log_sigmoid(x):
    return jnp.minimum(x, 0.0) - jnp.log1p(jnp.exp(-jnp.abs(x)))


def _dot_nt(a, b):
    return lax.dot_general(a, b, (((1,), (1,)), ((), ())), preferred_element_type=F32)


def _dot_tn(a, b):
    return lax.dot_general(a, b, (((0,), (0,)), ((), ())), preferred_element_type=F32)


def _mlstm_body(x_ref, c0_ref, n0_ref, m0_ref, g_ref, win_ref, wg_ref, bg_ref, gn_ref, wo_ref,
                o_ref, c_ref, n_ref, m_ref, *scratch, tl, L, per_chunk_state, sub):
    H = MLSTM_HEADS
    t = pl.program_id(1)
    n_sub = tl // sub
    p_ss, gate_ss, hn_ss = scratch[:n_sub], scratch[n_sub:2 * n_sub], scratch[2 * n_sub:]

    @pl.when(t == 0)
    def _():
        c_ref[...] = c0_ref[...]
        n_ref[...] = n0_ref[...]
        m_ref[...] = m0_ref[...]

    for i_sub in range(n_sub):
        x = x_ref[i_sub * sub:(i_sub + 1) * sub, :]
        u = _rmsnorm_f32(x, g_ref[...]).astype(BF16)
        p_ss[i_sub][...] = jnp.dot(u, win_ref[...], preferred_element_type=F32)
        pre = jnp.dot(u, wg_ref[...], preferred_element_type=F32) + bg_ref[...]
        lane = lax.broadcasted_iota(jnp.int32, pre.shape, 1)
        gate_ss[i_sub][...] = jnp.where(lane < H, pre, _log_sigmoid(pre))

    row_g = lax.broadcasted_iota(jnp.int32, (L, GATE_LANES), 0)
    causal = lax.broadcasted_iota(jnp.int32, (L, L), 0) >= lax.broadcasted_iota(jnp.int32, (L, L), 1)
    for c in range(tl // L):
        i_sub, r0 = divmod(c * L, sub)
        p_s, gate_s, hn_s = p_ss[i_sub], gate_ss[i_sub], hn_ss[i_sub]
        si = c if per_chunk_state else 0
        G = gate_s[r0:r0 + L, :]
        Bc = G
        k = 1
        while k < L:
            Bc = Bc + jnp.where(row_g >= k, pltpu.roll(Bc, k, 0), 0.0)
            k *= 2
        A = G - pltpu.roll(Bc, GATE_LANES - H, 1)
        AT = A.T
        for h in range(H):
            bcol = Bc[:, H + h:H + h + 1]
            acol = A[:, h:h + 1]
            arow = AT[h:h + 1, :]
            C = c_ref[si, h]
            n = n_ref[si, h]
            m_prev = m_ref[si, h][:, 0:1]
            q = p_s[r0:r0 + L, h * MLSTM_DQK:(h + 1) * MLSTM_DQK] * (MLSTM_DQK ** -0.5)
            kk = p_s[r0:r0 + L, MLSTM_QK + h * MLSTM_DQK:MLSTM_QK + (h + 1) * MLSTM_DQK]
            v0 = 2 * MLSTM_QK + h * MLSTM_DV
            v = p_s[r0:r0 + L, v0:v0 + MLSTM_DV]
            og = p_s[r0:r0 + L, v0 + MLSTM_V:v0 + MLSTM_V + MLSTM_DV]
            qb = q.astype(BF16)
            kb = kk.astype(BF16)

            log_d = jnp.where(causal, bcol + arow, -jnp.inf)
            log_state = bcol + m_prev
            m_t = jnp.maximum(jnp.max(log_d, axis=-1, keepdims=True), log_state)
            w_intra = jnp.exp(log_d - m_t)
            w_state = jnp.exp(log_state - m_t)
            s = _dot_nt(qb, kb) * w_intra
            num = (jnp.dot(s.astype(BF16), v.astype(BF16), preferred_element_type=F32)
                   + w_state * _dot_nt(qb, C.astype(BF16)))
            den = jnp.sum(s, axis=-1, keepdims=True) + w_state * jnp.sum(q * n, axis=-1, keepdims=True)
            hh = num / jnp.maximum(jnp.abs(den), jnp.exp(-m_t))
            hh = hh * lax.rsqrt(jnp.mean(hh * hh, axis=-1, keepdims=True) + EPS)
            hh = hh * gn_ref[:, h * MLSTM_DV:(h + 1) * MLSTM_DV]
            hn_s[r0:r0 + L, h * MLSTM_DV:(h + 1) * MLSTM_DV] = (jax.nn.sigmoid(og) * hh).astype(BF16)

            m_new = m_t[L - 1:L, :]
            b_last = bcol[L - 1:L, :]
            w_carry = jnp.exp(b_last + m_prev - m_new)
            w_write = jnp.exp(b_last + acol - m_new)
            c_ref[si, h] = w_carry * C + _dot_tn((v * w_write).astype(BF16), kb)
            n_ref[si, h] = w_carry * n + jnp.sum(kk * w_write, axis=0, keepdims=True)
            m_ref[si, h] = jnp.broadcast_to(m_new, (1, GATE_LANES))

        if r0 + L == sub:
            rows = slice(i_sub * sub, (i_sub + 1) * sub)
            o_ref[rows, :] = x_ref[rows, :] + jnp.dot(hn_s[...], wo_ref[...], preferred_element_type=F32)


def _mlstm_mixer(x, C0, n0, m0, g, w_main, w_gate, b_gate, g_norm, w_out,
                 *, layer, cell_layer, seq, tl, L, per_chunk_state):
    T = x.shape[0]
    H = MLSTM_HEADS
    B = C0.shape[1]
    jl = cell_layer
    sub = min(tl, SUB_ROWS)
    n_sub = tl // sub
    nb = B if per_chunk_state else 1
    tiles_per_seq = 1 if per_chunk_state else seq // tl
    if per_chunk_state:
        assert seq == L and tl == nb * L and T == tl
    else:
        assert seq % tl == 0 and T == B * seq
    assert tl % sub == 0 and sub % L == 0
    row = pl.BlockSpec((tl, D_MODEL), lambda b, t: (b * tiles_per_seq + t, 0))
    c_spec = pl.BlockSpec((nb, H, MLSTM_DV, MLSTM_DQK), lambda b, t: (b, 0, 0, 0))
    n_spec = pl.BlockSpec((nb, H, 1, MLSTM_DQK), lambda b, t: (b, 0, 0, 0))
    m_spec = pl.BlockSpec((nb, H, 1, GATE_LANES), lambda b, t: (b, 0, 0, 0))
    return pl.pallas_call(
        functools.partial(_mlstm_body, tl=tl, L=L, per_chunk_state=per_chunk_state, sub=sub),
        grid=(B // nb, tiles_per_seq),
        in_specs=[
            row,
            pl.BlockSpec((None, nb, H, MLSTM_DV, MLSTM_DQK), lambda b, t: (jl, b, 0, 0, 0)),
            pl.BlockSpec((None, nb, H, 1, MLSTM_DQK), lambda b, t: (jl, b, 0, 0, 0)),
            pl.BlockSpec((None, nb, H, 1, GATE_LANES), lambda b, t: (jl, b, 0, 0, 0)),
            _resident((1, D_MODEL), layer),
            _resident((D_MODEL, MLSTM_MAIN), jl),
            _resident((D_MODEL, GATE_LANES), jl),
            _resident((1, GATE_LANES), jl),
            _resident((1, MLSTM_V), jl),
            _resident((MLSTM_V, D_MODEL), jl),
        ],
        out_specs=[row, c_spec, n_spec, m_spec],
        out_shape=[
            jax.ShapeDtypeStruct((T, D_MODEL), F32),
            jax.ShapeDtypeStruct((B, H, MLSTM_DV, MLSTM_DQK), F32),
            jax.ShapeDtypeStruct((B, H, 1, MLSTM_DQK), F32),
            jax.ShapeDtypeStruct((B, H, 1, GATE_LANES), F32),
        ],
        scratch_shapes=([pltpu.VMEM((sub, MLSTM_MAIN), F32)] * n_sub
                        + [pltpu.VMEM((sub, GATE_LANES), F32)] * n_sub
                        + [pltpu.VMEM((sub, MLSTM_V), BF16)] * n_sub),
        compiler_params=pltpu.CompilerParams(
            dimension_semantics=("arbitrary", "arbitrary"), vmem_limit_bytes=VMEM_LIMIT_BYTES),
        name="mlstm_layer",
    )(x, C0, n0, m0, g, w_main, w_gate, b_gate, g_norm, w_out)


def _trunk(x, pool_hist, mC, mn, mm, pos0, W, ffn_w, *, tm, ts, tl, L, per_chunk_state, convert):
    B, S, _ = x.shape
    pool_new, C_new, n_new, m_new = [], [], [], []
    hist = jnp.pad(pool_hist, ((0, 0), (0, 0), (1, 0), (0, 0)))
    n0 = mn[:, :, :, None, :]
    m0 = jnp.broadcast_to(mm[:, :, :, None, None], mm.shape + (1, GATE_LANES))
    ffn_calls = [(name, i) for i in range(DEPTH) for name in ('ffn1', 'ffn2')]

    def run_ffn(k, x2):
        name, layer = ffn_calls[k]
        cv = None
        if convert and k + 1 < len(ffn_calls):
            nxt, nxt_layer = ffn_calls[k + 1]
            cv = (W[nxt + '_w_in'], W[nxt + '_w_out'], nxt_layer)
        out = _ffn(x2, W['norm_' + name], ffn_w[k][0], ffn_w[k][1], W['norm_final'], layer=layer,
                   final_norm=(k == len(ffn_calls) - 1), tm=tm, convert=cv)
        if cv is None:
            return out
        ffn_w.append((out[1], out[2]))
        return out[0]

    x2 = x.reshape(B * S, D_MODEL)
    for i in range(DEPTH):
        j = i // 2
        x2 = run_ffn(2 * i, x2)
        if i % 2 == 0:
            x3, hist_new = _pool_mixer(x2.reshape(B, S, D_MODEL), hist, W['norm_mix'], W['pool_w'],
                                       W['pool_b'], W['pool_scale'], layer=i, pool_layer=j, pos0=pos0, ts=ts)
            x2 = x3.reshape(B * S, D_MODEL)
            pool_new.append(hist_new[:, 1:])
        else:
            x2, C, n, m = _mlstm_mixer(x2, mC, n0, m0, W['norm_mix'], W['mlstm_w_in'], W['mlstm_w_gate'],
                                       W['mlstm_b_gate'], W['mlstm_norm'], W['mlstm_w_out'],
                                       layer=i, cell_layer=j, seq=S, tl=tl, L=L, per_chunk_state=per_chunk_state)
            C_new.append(C)
            n_new.append(n[:, :, 0, :])
            m_new.append(m[:, :, 0, 0])
        x2 = run_ffn(2 * i + 1, x2)
    return (x2.reshape(B, S, D_MODEL), jnp.stack(pool_new), jnp.stack(C_new), jnp.stack(n_new),
            jnp.stack(m_new))


def kernel(x_prompt, x_sample, state_pool, state_mlstm_C, state_mlstm_n, state_mlstm_m, norm_ffn1,
           ffn1_w_in, ffn1_w_out, norm_mix, pool_w, pool_b, pool_scale, mlstm_w_in, mlstm_b_i, mlstm_b_f,
           mlstm_norm, mlstm_w_out, norm_ffn2, ffn2_w_in, ffn2_w_out, norm_final):
    n_gate = 2 * MLSTM_HEADS
    W = dict(
        norm_ffn1=norm_ffn1[:, None, :], norm_ffn2=norm_ffn2[:, None, :], norm_mix=norm_mix[:, None, :],
        norm_final=norm_final[None, None, :],
        ffn1_w_in=ffn1_w_in, ffn1_w_out=ffn1_w_out, ffn2_w_in=ffn2_w_in, ffn2_w_out=ffn2_w_out,
        pool_w=pool_w.astype(BF16), pool_b=pool_b[:, None, :], pool_scale=pool_scale[:, None, :],
        mlstm_w_in=mlstm_w_in.astype(BF16),
        mlstm_w_gate=jnp.pad(mlstm_w_in[:, :, MLSTM_MAIN:], ((0, 0), (0, 0), (0, GATE_LANES - n_gate))).astype(BF16),
        mlstm_b_gate=jnp.pad(jnp.concatenate([mlstm_b_i, mlstm_b_f], axis=-1),
                             ((0, 0), (0, GATE_LANES - n_gate)))[:, None, :],
        mlstm_norm=mlstm_norm[:, None, :], mlstm_w_out=mlstm_w_out.astype(BF16),
    )
    B = x_prompt.shape[0]
    zero_pool = jnp.zeros((state_pool.shape[0], B) + state_pool.shape[2:], state_pool.dtype)
    zero_C = jnp.zeros((state_mlstm_C.shape[0], B) + state_mlstm_C.shape[2:], state_mlstm_C.dtype)
    zero_n = jnp.zeros((state_mlstm_n.shape[0], B) + state_mlstm_n.shape[2:], state_mlstm_n.dtype)
    zero_m = jnp.zeros((state_mlstm_m.shape[0], B) + state_mlstm_m.shape[2:], state_mlstm_m.dtype)
    ffn_w = [(ffn1_w_in[0].astype(BF16), ffn1_w_out[0].astype(BF16))]
    y_p, pool_p, C_p, n_p, m_p = _trunk(x_prompt, zero_pool, zero_C, zero_n, zero_m, 0, W, ffn_w,
                                        tm=1024, ts=512, tl=1024, L=256, per_chunk_state=False, convert=True)
    Bs, Ss, _ = x_sample.shape
    y_s, pool_s, C_s, n_s, m_s = _trunk(x_sample, state_pool, state_mlstm_C, state_mlstm_n, state_mlstm_m,
                                        PAST_LEN, W, ffn_w, tm=Bs * Ss, ts=Ss, tl=Bs * Ss, L=Ss,
                                        per_chunk_state=True, convert=False)
    return (y_p, y_s, pool_p, C_p, n_p, m_p, pool_s, C_s, n_s, m_s)
```

```python
import functools

import jax
import jax.numpy as jnp
from jax import lax
from jax.experimental import pallas as pl
from jax.experimental.pallas import tpu as pltpu

D_MODEL = 1024
D_FF = 2816
DEPTH = 4
PAST_LEN = 1024
POOL_WINDOWS = (2, 4, 8, 16)
POOL_GROUP_WIDTH = D_MODEL // len(POOL_WINDOWS)
POOL_HIST = max(POOL_WINDOWS) - 1
HIST_ROWS = POOL_HIST + 1
MLSTM_HEADS = 4
MLSTM_DQK = D_MODEL // (2 * MLSTM_HEADS)
MLSTM_DV = D_MODEL // MLSTM_HEADS
MLSTM_QK = MLSTM_HEADS * MLSTM_DQK
MLSTM_V = MLSTM_HEADS * MLSTM_DV
MLSTM_MAIN = 2 * MLSTM_QK + 2 * MLSTM_V
GATE_LANES = 128
EPS = 1e-6

VMEM_LIMIT_BYTES = 56 * 1024 * 1024
FFN_SUB_ROWS = 256
MLSTM_SUB_ROWS = 512

F32 = jnp.float32
BF16 = jnp.bfloat16


def _rmsnorm_f32(x, g):
    return x * lax.rsqrt(jnp.mean(x * x, axis=-1, keepdims=True) + EPS) * g


def _resident(shape, layer, col=0):
    idx = (layer,) + (0,) * (len(shape) - 1) + (col,)
    return pl.BlockSpec((None,) + tuple(shape), lambda *_: idx, pipeline_mode=pl.Buffered(1))


def _pool_rows(x, head, ext_ref, pos_first, g, w_ref, b, sc):
    n = x.shape[0]
    u = _rmsnorm_f32(x, g)
    ext_ref[0:HIST_ROWS, :] = head
    ext_ref[HIST_ROWS:HIST_ROWS + n, :] = u
    ext = ext_ref[...]
    pos = pos_first + lax.broadcasted_iota(jnp.int32, (n, 1), 0)
    ys = []
    for gi, w in enumerate(POOL_WINDOWS):
        c0 = gi * POOL_GROUP_WIDTH
        s = ext[:, c0:c0 + POOL_GROUP_WIDTH]
        k = 1
        while k < w:
            s = s + pltpu.roll(s, k, 0)
            k *= 2
        cnt = jnp.minimum(pos + 1, w).astype(F32)
        d = s[HIST_ROWS:, :] / cnt - u[:, c0:c0 + POOL_GROUP_WIDTH]
        ys.append(jnp.dot(d.astype(BF16), w_ref[gi], preferred_element_type=F32))
    y = (jnp.concatenate(ys, axis=-1) + b) * sc
    return x + y, ext_ref[n:n + HIST_ROWS, :]


def _convert_slab(src_ref, dst_ref, in_buf, out_buf, in_sem, out_sem, *, step, n_steps):
    rows = in_buf.shape[1]
    slot = step % 2

    def in_copy(k, s):
        return pltpu.make_async_copy(src_ref.at[pl.ds(k * rows, rows), :], in_buf.at[s], in_sem.at[s])

    def out_copy(k, s):
        return pltpu.make_async_copy(out_buf.at[s], dst_ref.at[pl.ds(k * rows, rows), :], out_sem.at[s])

    @pl.when(step == 0)
    def _():
        in_copy(0, 0).start()

    in_copy(step, slot).wait()

    @pl.when(step + 1 < n_steps)
    def _():
        in_copy(step + 1, 1 - slot).start()

    @pl.when(step >= 2)
    def _():
        out_copy(step - 2, slot).wait()

    out_buf[slot] = in_buf[slot].astype(BF16)

    def finish():
        out_copy(step, slot).start()

        @pl.when(step == n_steps - 1)
        def _():
            if n_steps >= 2:
                out_copy(step - 1, 1 - slot).wait()
            out_copy(step, slot).wait()

    return finish


def _ffn_body(*refs, final_norm, sub, n_steps, convert_layer, pool):
    it = iter(refs)
    take = lambda n: [next(it) for _ in range(n)]
    do_convert = convert_layer is not None
    x_ref, g_ref, wa_ref, wb_ref, wo_ref, gf_ref = take(6)
    hist_ref, gm_ref, pw_ref, pb_ref, ps_ref = take(5) if pool else [None] * 5
    nwi_ref, nwo_ref = take(2) if do_convert else [None] * 2
    (o_ref,) = take(1)
    (hist_out_ref,) = take(1) if pool else [None]
    cwi_ref, cwo_ref = take(2) if do_convert else [None] * 2
    ext_ref, carry_ref = take(2) if pool else [None] * 2
    step = pl.program_id(0)
    tm = x_ref.shape[0]
    if pool:
        tiles_per_seq, pos0 = pool
        t_in_seq = step % tiles_per_seq

        @pl.when(t_in_seq == 0)
        def _():
            carry_ref[...] = hist_ref[0]

    finishers = []
    if do_convert:
        wi_in, wi_out, wo_in, wo_out, sem_ii, sem_io, sem_oi, sem_oo = take(8)
        finishers = [
            _convert_slab(nwi_ref.at[convert_layer], cwi_ref, wi_in, wi_out, sem_ii, sem_io, step=step, n_steps=n_steps),
            _convert_slab(nwo_ref.at[convert_layer], cwo_ref, wo_in, wo_out, sem_oi, sem_oo, step=step, n_steps=n_steps),
        ]
    def pool_and_store(r0, y):
        y, tail = _pool_rows(y, carry_ref[...], ext_ref, pos0 + t_in_seq * tm + r0, gm_ref[...], pw_ref,
                             pb_ref[...], ps_ref[...])
        carry_ref[...] = tail
        hist_out_ref[0] = tail
        o_ref[r0:r0 + sub, :] = y

    deferred = None
    for r0 in range(0, tm, sub):
        x = x_ref[r0:r0 + sub, :]
        xn = _rmsnorm_f32(x, g_ref[...]).astype(BF16)
        a = jnp.dot(xn, wa_ref[...], preferred_element_type=F32)
        b = jnp.dot(xn, wb_ref[...], preferred_element_type=F32)
        h = (a * jax.nn.sigmoid(a) * b).astype(BF16)
        y = x + 0.5 * jnp.dot(h, wo_ref[...], preferred_element_type=F32)
        if final_norm:
            y = _rmsnorm_f32(y, gf_ref[...])
        if pool:
            if deferred is not None:
                pool_and_store(*deferred)
            deferred = (r0, y)
        else:
            o_ref[r0:r0 + sub, :] = y
    if deferred is not None:
        pool_and_store(*deferred)
    for finish in finishers:
        finish()


def _ffn(x, g, w_in, w_out, g_final, *, layer, final_norm, tm, convert=None, pool=None):
    T = x.shape[0]
    assert T % tm == 0
    n_steps = T // tm
    sub = min(tm, FFN_SUB_ROWS)
    row = pl.BlockSpec((tm, D_MODEL), lambda i: (i, 0))
    in_specs = [
        row,
        _resident((1, D_MODEL), layer),
        pl.BlockSpec((D_MODEL, D_FF), lambda i: (0, 0), pipeline_mode=pl.Buffered(1)),
        pl.BlockSpec((D_MODEL, D_FF), lambda i: (0, 1), pipeline_mode=pl.Buffered(1)),
        pl.BlockSpec((D_FF, D_MODEL), lambda i: (0, 0), pipeline_mode=pl.Buffered(1)),
        _resident((1, D_MODEL), 0),
    ]
    out_specs = [row]
    out_shape = [jax.ShapeDtypeStruct((T, D_MODEL), F32)]
    operands = [x, g, w_in, w_in, w_out, g_final]
    scratch = []
    pool_cfg = None
    if pool is not None:
        hist, g_mix, pw, pb, psc, pool_layer, seq, pos0 = pool
        assert seq % tm == 0 and sub >= HIST_ROWS
        tiles_per_seq = seq // tm
        pool_cfg = (tiles_per_seq, pos0)
        in_specs += [
            pl.BlockSpec((None, 1, HIST_ROWS, D_MODEL), lambda i: (pool_layer, i // tiles_per_seq, 0, 0)),
            _resident((1, D_MODEL), layer),
            _resident(pw.shape[1:], pool_layer),
            _resident((1, D_MODEL), pool_layer),
            _resident((1, D_MODEL), pool_layer),
        ]
        out_specs += [pl.BlockSpec((1, HIST_ROWS, D_MODEL), lambda i: (i // tiles_per_seq, 0, 0))]
        out_shape += [jax.ShapeDtypeStruct((T // seq, HIST_ROWS, D_MODEL), F32)]
        operands += [hist, g_mix, pw, pb, psc]
        scratch += [pltpu.VMEM((HIST_ROWS + sub, D_MODEL), F32), pltpu.VMEM((HIST_ROWS, D_MODEL), F32)]
    convert_layer = None
    if convert is not None:
        src_in, src_out, convert_layer = convert
        assert D_MODEL % n_steps == 0 and D_FF % n_steps == 0
        r_in, r_out = D_MODEL // n_steps, D_FF // n_steps
        any_spec = pl.BlockSpec(memory_space=pl.ANY)
        in_specs += [any_spec, any_spec]
        out_specs += [any_spec, any_spec]
        out_shape += [jax.ShapeDtypeStruct((D_MODEL, 2 * D_FF), BF16), jax.ShapeDtypeStruct((D_FF, D_MODEL), BF16)]
        operands += [src_in, src_out]
        scratch += [
            pltpu.VMEM((2, r_in, 2 * D_FF), F32), pltpu.VMEM((2, r_in, 2 * D_FF), BF16),
            pltpu.VMEM((2, r_out, D_MODEL), F32), pltpu.VMEM((2, r_out, D_MODEL), BF16),
        ] + [pltpu.SemaphoreType.DMA((2,))] * 4
    return pl.pallas_call(
        functools.partial(_ffn_body, final_norm=final_norm, sub=sub, n_steps=n_steps,
                          convert_layer=convert_layer, pool=pool_cfg),
        grid=(n_steps,),
        in_specs=in_specs,
        out_specs=out_specs,
        out_shape=out_shape,
        scratch_shapes=scratch,
        compiler_params=pltpu.CompilerParams(
            dimension_semantics=("arbitrary",), vmem_limit_bytes=VMEM_LIMIT_BYTES),
        name="ffn",
    )(*operands)


def _pool_body(x_ref, hist_ref, g_ref, w_ref, b_ref, sc_ref, o_ref, hist_out_ref, ext_ref, carry_ref,
               *, ts, pos0):
    t = pl.program_id(1)

    @pl.when(t == 0)
    def _():
        carry_ref[...] = hist_ref[0]

    y, tail = _pool_rows(x_ref[0], carry_ref[...], ext_ref, pos0 + t * ts, g_ref[...], w_ref, b_ref[...],
                         sc_ref[...])
    o_ref[0] = y
    carry_ref[...] = tail
    hist_out_ref[0] = tail


def _pool_mixer(x, hist, g, w, b, sc, *, layer, pool_layer, pos0, ts):
    B, S, _ = x.shape
    assert S % ts == 0 and ts >= HIST_ROWS
    return pl.pallas_call(
        functools.partial(_pool_body, ts=ts, pos0=pos0),
        grid=(B, S // ts),
        in_specs=[
            pl.BlockSpec((1, ts, D_MODEL), lambda bi, t: (bi, t, 0)),
            pl.BlockSpec((None, 1, HIST_ROWS, D_MODEL), lambda bi, t: (pool_layer, bi, 0, 0)),
            _resident((1, D_MODEL), layer),
            _resident(w.shape[1:], pool_layer),
            _resident((1, D_MODEL), pool_layer),
            _resident((1, D_MODEL), pool_layer),
        ],
        out_specs=[
            pl.BlockSpec((1, ts, D_MODEL), lambda bi, t: (bi, t, 0)),
            pl.BlockSpec((1, HIST_ROWS, D_MODEL), lambda bi, t: (bi, 0, 0)),
        ],
        out_shape=[
            jax.ShapeDtypeStruct((B, S, D_MODEL), F32),
            jax.ShapeDtypeStruct((B, HIST_ROWS, D_MODEL), F32),
        ],
        scratch_shapes=[pltpu.VMEM((HIST_ROWS + ts, D_MODEL), F32), pltpu.VMEM((HIST_ROWS, D_MODEL), F32)],
        compiler_params=pltpu.CompilerParams(
            dimension_semantics=("arbitrary", "arbitrary"), vmem_limit_bytes=VMEM_LIMIT_BYTES),
        name="pool_mixer",
    )(x, hist, g, w, b, sc)


def _log_sigmoid(x):
    return jnp.minimum(x, 0.0) - jnp.log1p(jnp.exp(-jnp.abs(x)))


def _dot_nt(a, b):
    return lax.dot_general(a, b, (((1,), (1,)), ((), ())), preferred_element_type=F32)


def _dot_tn(a, b):
    return lax.dot_general(a, b, (((0,), (0,)), ((), ())), preferred_element_type=F32)


def _mlstm_body(x_ref, c0_ref, n0_ref, m0_ref, g_ref, win_ref, wg_ref, bg_ref, gn_ref, wo_ref,
                o_ref, c_ref, n_ref, m_ref, *scratch, tl, L, per_chunk_state, sub):
    H = MLSTM_HEADS
    t = pl.program_id(1)
    n_sub = tl // sub
    p_ss, gate_ss, hn_ss = scratch[:n_sub], scratch[n_sub:2 * n_sub], scratch[2 * n_sub:]

    @pl.when(t == 0)
    def _():
        c_ref[...] = c0_ref[...]
        n_ref[...] = n0_ref[...]
        m_ref[...] = m0_ref[...]

    for i_sub in range(n_sub):
        x = x_ref[i_sub * sub:(i_sub + 1) * sub, :]
        u = _rmsnorm_f32(x, g_ref[...]).astype(BF16)
        p_ss[i_sub][...] = jnp.dot(u, win_ref[...], preferred_element_type=F32)
        pre = jnp.dot(u, wg_ref[...], preferred_element_type=F32) + bg_ref[...]
        lane = lax.broadcasted_iota(jnp.int32, pre.shape, 1)
        gate_ss[i_sub][...] = jnp.where(lane < H, pre, _log_sigmoid(pre))

    row_g = lax.broadcasted_iota(jnp.int32, (L, GATE_LANES), 0)
    causal = lax.broadcasted_iota(jnp.int32, (L, L), 0) >= lax.broadcasted_iota(jnp.int32, (L, L), 1)
    for c in range(tl // L):
        i_sub, r0 = divmod(c * L, sub)
        p_s, gate_s, hn_s = p_ss[i_sub], gate_ss[i_sub], hn_ss[i_sub]
        si = c if per_chunk_state else 0
        G = gate_s[r0:r0 + L, :]
        Bc = G
        k = 1
        while k < L:
            Bc = Bc + jnp.where(row_g >= k, pltpu.roll(Bc, k, 0), 0.0)
            k *= 2
        A = G - pltpu.roll(Bc, GATE_LANES - H, 1)
        AT = A.T
        for h in range(H):
            bcol = Bc[:, H + h:H + h + 1]
            acol = A[:, h:h + 1]
            arow = AT[h:h + 1, :]
            C = c_ref[si, h]
            n = n_ref[si, h]
            m_prev = m_ref[si, h][:, 0:1]
            q = p_s[r0:r0 + L, h * MLSTM_DQK:(h + 1) * MLSTM_DQK] * (MLSTM_DQK ** -0.5)
            kk = p_s[r0:r0 + L, MLSTM_QK + h * MLSTM_DQK:MLSTM_QK + (h + 1) * MLSTM_DQK]
            v0 = 2 * MLSTM_QK + h * MLSTM_DV
            v = p_s[r0:r0 + L, v0:v0 + MLSTM_DV]
            og = p_s[r0:r0 + L, v0 + MLSTM_V:v0 + MLSTM_V + MLSTM_DV]
            qb = q.astype(BF16)
            kb = kk.astype(BF16)

            log_d = jnp.where(causal, bcol + arow, -jnp.inf)
            log_state = bcol + m_prev
            m_t = jnp.maximum(jnp.max(log_d, axis=-1, keepdims=True), log_state)
            w_intra = jnp.exp(log_d - m_t)
            w_state = jnp.exp(log_state - m_t)
            s = _dot_nt(qb, kb) * w_intra
            num = (jnp.dot(s.astype(BF16), v.astype(BF16), preferred_element_type=F32)
                   + w_state * _dot_nt(qb, C.astype(BF16)))
            den = jnp.sum(s, axis=-1, keepdims=True) + w_state * jnp.sum(q * n, axis=-1, keepdims=True)
            hh = num / jnp.maximum(jnp.abs(den), jnp.exp(-m_t))
            hh = hh * lax.rsqrt(jnp.mean(hh * hh, axis=-1, keepdims=True) + EPS)
            hh = hh * gn_ref[:, h * MLSTM_DV:(h + 1) * MLSTM_DV]
            hn_s[r0:r0 + L, h * MLSTM_DV:(h + 1) * MLSTM_DV] = (jax.nn.sigmoid(og) * hh).astype(BF16)

            m_new = m_t[L - 1:L, :]
            b_last = bcol[L - 1:L, :]
            w_carry = jnp.exp(b_last + m_prev - m_new)
            w_write = jnp.exp(b_last + acol - m_new)
            c_ref[si, h] = w_carry * C + _dot_tn((v * w_write).astype(BF16), kb)
            n_ref[si, h] = w_carry * n + jnp.sum(kk * w_write, axis=0, keepdims=True)
            m_ref[si, h] = jnp.broadcast_to(m_new, (1, GATE_LANES))

        if r0 + L == sub:
            rows = slice(i_sub * sub, (i_sub + 1) * sub)
            o_ref[rows, :] = x_ref[rows, :] + jnp.dot(hn_s[...], wo_ref[...], preferred_element_type=F32)


def _mlstm_mixer(x, C0, n0, m0, g, w_main, w_gate, b_gate, g_norm, w_out,
                 *, layer, cell_layer, seq, tl, L, per_chunk_state):
    T = x.shape[0]
    H = MLSTM_HEADS
    B = C0.shape[1]
    jl = cell_layer
    sub = min(tl, MLSTM_SUB_ROWS)
    n_sub = tl // sub
    nb = B if per_chunk_state else 1
    tiles_per_seq = 1 if per_chunk_state else seq // tl
    if per_chunk_state:
        assert seq == L and tl == nb * L and T == tl
    else:
        assert seq % tl == 0 and T == B * seq
    assert tl % sub == 0 and sub % L == 0
    row = pl.BlockSpec((tl, D_MODEL), lambda b, t: (b * tiles_per_seq + t, 0))
    c_spec = pl.BlockSpec((nb, H, MLSTM_DV, MLSTM_DQK), lambda b, t: (b, 0, 0, 0))
    n_spec = pl.BlockSpec((nb, H, 1, MLSTM_DQK), lambda b, t: (b, 0, 0, 0))
    m_spec = pl.BlockSpec((nb, H, 1, GATE_LANES), lambda b, t: (b, 0, 0, 0))
    return pl.pallas_call(
        functools.partial(_mlstm_body, tl=tl, L=L, per_chunk_state=per_chunk_state, sub=sub),
        grid=(B // nb, tiles_per_seq),
        in_specs=[
            row,
            pl.BlockSpec((None, nb, H, MLSTM_DV, MLSTM_DQK), lambda b, t: (jl, b, 0, 0, 0)),
            pl.BlockSpec((None, nb, H, 1, MLSTM_DQK), lambda b, t: (jl, b, 0, 0, 0)),
            pl.BlockSpec((None, nb, H, 1, GATE_LANES), lambda b, t: (jl, b, 0, 0, 0)),
            _resident((1, D_MODEL), layer),
            _resident((D_MODEL, MLSTM_MAIN), jl),
            _resident((D_MODEL, GATE_LANES), jl),
            _resident((1, GATE_LANES), jl),
            _resident((1, MLSTM_V), jl),
            _resident((MLSTM_V, D_MODEL), jl),
        ],
        out_specs=[row, c_spec, n_spec, m_spec],
        out_shape=[
            jax.ShapeDtypeStruct((T, D_MODEL), F32),
            jax.ShapeDtypeStruct((B, H, MLSTM_DV, MLSTM_DQK), F32),
            jax.ShapeDtypeStruct((B, H, 1, MLSTM_DQK), F32),
            jax.ShapeDtypeStruct((B, H, 1, GATE_LANES), F32),
        ],
        scratch_shapes=([pltpu.VMEM((sub, MLSTM_MAIN), F32)] * n_sub
                        + [pltpu.VMEM((sub, GATE_LANES), F32)] * n_sub
                        + [pltpu.VMEM((sub, MLSTM_V), BF16)] * n_sub),
        compiler_params=pltpu.CompilerParams(
            dimension_semantics=("arbitrary", "arbitrary"), vmem_limit_bytes=VMEM_LIMIT_BYTES),
        name="mlstm_layer",
    )(x, C0, n0, m0, g, w_main, w_gate, b_gate, g_norm, w_out)


def _trunk(x, pool_hist, mC, mn, mm, pos0, W, ffn_w, *, tm, ts, tl, L, per_chunk_state, convert, fuse_pool):
    B, S, _ = x.shape
    pool_new, C_new, n_new, m_new = [], [], [], []
    hist = jnp.pad(pool_hist, ((0, 0), (0, 0), (1, 0), (0, 0)))
    n0 = mn[:, :, :, None, :]
    m0 = jnp.broadcast_to(mm[:, :, :, None, None], mm.shape + (1, GATE_LANES))
    ffn_calls = [(name, i) for i in range(DEPTH) for name in ('ffn1', 'ffn2')]

    def run_ffn(k, x2, pool=None):
        name, layer = ffn_calls[k]
        cv = None
        if convert and k + 1 < len(ffn_calls):
            nxt, nxt_layer = ffn_calls[k + 1]
            cv = (W[nxt + '_w_in'], W[nxt + '_w_out'], nxt_layer)
        out = _ffn(x2, W['norm_' + name], ffn_w[k][0], ffn_w[k][1], W['norm_final'], layer=layer,
                   final_norm=(k == len(ffn_calls) - 1), tm=tm, convert=cv, pool=pool)
        if cv is not None:
            ffn_w.append((out[-2], out[-1]))
        return out[0], (out[1] if pool is not None else None)

    x2 = x.reshape(B * S, D_MODEL)
    for i in range(DEPTH):
        j = i // 2
        if i % 2 == 0 and fuse_pool:
            x2, hist_new = run_ffn(2 * i, x2, pool=(hist, W['norm_mix'], W['pool_w'], W['pool_b'], W['pool_scale'],
                                                    j, S, pos0))
            pool_new.append(hist_new[:, 1:])
        elif i % 2 == 0:
            x2, _ = run_ffn(2 * i, x2)
            x3, hist_new = _pool_mixer(x2.reshape(B, S, D_MODEL), hist, W['norm_mix'], W['pool_w'],
                                       W['pool_b'], W['pool_scale'], layer=i, pool_layer=j, pos0=pos0, ts=ts)
            x2 = x3.reshape(B * S, D_MODEL)
            pool_new.append(hist_new[:, 1:])
        else:
            x2, _ = run_ffn(2 * i, x2)
            x2, C, n, m = _mlstm_mixer(x2, mC, n0, m0, W['norm_mix'], W['mlstm_w_in'], W['mlstm_w_gate'],
                                       W['mlstm_b_gate'], W['mlstm_norm'], W['mlstm_w_out'],
                                       layer=i, cell_layer=j, seq=S, tl=tl, L=L, per_chunk_state=per_chunk_state)
            C_new.append(C)
            n_new.append(n[:, :, 0, :])
            m_new.append(m[:, :, 0, 0])
        x2, _ = run_ffn(2 * i + 1, x2)
    return (x2.reshape(B, S, D_MODEL), jnp.stack(pool_new), jnp.stack(C_new), jnp.stack(n_new),
            jnp.stack(m_new))


def kernel(x_prompt, x_sample, state_pool, state_mlstm_C, state_mlstm_n, state_mlstm_m, norm_ffn1,
           ffn1_w_in, ffn1_w_out, norm_mix, pool_w, pool_b, pool_scale, mlstm_w_in, mlstm_b_i, mlstm_b_f,
           mlstm_norm, mlstm_w_out, norm_ffn2, ffn2_w_in, ffn2_w_out, norm_final):
    n_gate = 2 * MLSTM_HEADS
    W = dict(
        norm_ffn1=norm_ffn1[:, None, :], norm_ffn2=norm_ffn2[:, None, :], norm_mix=norm_mix[:, None, :],
        norm_final=norm_final[None, None, :],
        ffn1_w_in=ffn1_w_in, ffn1_w_out=ffn1_w_out, ffn2_w_in=ffn2_w_in, ffn2_w_out=ffn2_w_out,
        pool_w=pool_w.astype(BF16), pool_b=pool_b[:, None, :], pool_scale=pool_scale[:, None, :],
        mlstm_w_in=mlstm_w_in.astype(BF16),
        mlstm_w_gate=jnp.pad(mlstm_w_in[:, :, MLSTM_MAIN:], ((0, 0), (0, 0), (0, GATE_LANES - n_gate))).astype(BF16),
        mlstm_b_gate=jnp.pad(jnp.concatenate([mlstm_b_i, mlstm_b_f], axis=-1),
                             ((0, 0), (0, GATE_LANES - n_gate)))[:, None, :],
        mlstm_norm=mlstm_norm[:, None, :], mlstm_w_out=mlstm_w_out.astype(BF16),
    )
    B = x_prompt.shape[0]
    zero_pool = jnp.zeros((state_pool.shape[0], B) + state_pool.shape[2:], state_pool.dtype)
    zero_C = jnp.zeros((state_mlstm_C.shape[0], B) + state_mlstm_C.shape[2:], state_mlstm_C.dtype)
    zero_n = jnp.zeros((state_mlstm_n.shape[0], B) + state_mlstm_n.shape[2:], state_mlstm_n.dtype)
    zero_m = jnp.zeros((state_mlstm_m.shape[0], B) + state_mlstm_m.shape[2:], state_mlstm_m.dtype)
    ffn_w = [(ffn1_w_in[0].astype(BF16), ffn1_w_out[0].astype(BF16))]
    y_p, pool_p, C_p, n_p, m_p = _trunk(x_prompt, zero_pool, zero_C, zero_n, zero_m, 0, W, ffn_w,
                                        tm=1024, ts=512, tl=1024, L=256, per_chunk_state=False, convert=True,
                                        fuse_pool=True)
    Bs, Ss, _ = x_sample.shape
    y_s, pool_s, C_s, n_s, m_s = _trunk(x_sample, state_pool, state_mlstm_C, state_mlstm_n, state_mlstm_m,
                                        PAST_LEN, W, ffn_w, tm=Bs * Ss, ts=Ss, tl=Bs * Ss, L=Ss,
                                        per_chunk_state=True, convert=False, fuse_pool=False)
    return (y_p, y_s, pool_p, C_p, n_p, m_p, pool_s, C_s, n_s, m_s)
```

```python
import functools

import jax
import jax.numpy as jnp
from jax import lax
from jax.experimental import pallas as pl
from jax.experimental.pallas import tpu as pltpu

D_MODEL = 1024
D_FF = 2816
DEPTH = 4
PAST_LEN = 1024
POOL_WINDOWS = (2, 4, 8, 16)
POOL_GROUP_WIDTH = D_MODEL // len(POOL_WINDOWS)
POOL_HIST = max(POOL_WINDOWS) - 1
HIST_ROWS = POOL_HIST + 1
MLSTM_HEADS = 4
MLSTM_DQK = D_MODEL // (2 * MLSTM_HEADS)
MLSTM_DV = D_MODEL // MLSTM_HEADS
MLSTM_QK = MLSTM_HEADS * MLSTM_DQK
MLSTM_V = MLSTM_HEADS * MLSTM_DV
MLSTM_MAIN = 2 * MLSTM_QK + 2 * MLSTM_V
GATE_LANES = 128
EPS = 1e-6

VMEM_LIMIT_BYTES = 56 * 1024 * 1024
FFN_SUB_ROWS = 256
MLSTM_SUB_ROWS = 512
FFN_TILE_ROWS = 1024
MLSTM_TILE_ROWS = 1024
MLSTM_CHUNK = 256

F32 = jnp.float32
BF16 = jnp.bfloat16


def _rmsnorm_f32(x, g):
    return x * lax.rsqrt(jnp.mean(x * x, axis=-1, keepdims=True) + EPS) * g


def _resident(shape, layer, col=0):
    idx = (layer,) + (0,) * (len(shape) - 1) + (col,)
    return pl.BlockSpec((None,) + tuple(shape), lambda *_: idx, pipeline_mode=pl.Buffered(1))


def _pool_rows(x, head, ext_ref, pos_first, g, w_ref, b, sc):
    n = x.shape[0]
    u = _rmsnorm_f32(x, g)
    ext_ref[0:HIST_ROWS, :] = head
    ext_ref[HIST_ROWS:HIST_ROWS + n, :] = u
    ext = ext_ref[...]
    pos = pos_first + lax.broadcasted_iota(jnp.int32, (n, 1), 0)
    ys = []
    for gi, w in enumerate(POOL_WINDOWS):
        c0 = gi * POOL_GROUP_WIDTH
        s = ext[:, c0:c0 + POOL_GROUP_WIDTH]
        k = 1
        while k < w:
            s = s + pltpu.roll(s, k, 0)
            k *= 2
        cnt = jnp.minimum(pos + 1, w).astype(F32)
        d = s[HIST_ROWS:, :] / cnt - u[:, c0:c0 + POOL_GROUP_WIDTH]
        ys.append(jnp.dot(d.astype(BF16), w_ref[gi], preferred_element_type=F32))
    y = (jnp.concatenate(ys, axis=-1) + b) * sc
    return x + y, ext_ref[n:n + HIST_ROWS, :]


def _convert_slab(src_ref, dst_ref, in_buf, out_buf, in_sem, out_sem, *, step, n_steps):
    rows = in_buf.shape[1]
    slot = step % 2

    def in_copy(k, s):
        return pltpu.make_async_copy(src_ref.at[pl.ds(k * rows, rows), :], in_buf.at[s], in_sem.at[s])

    def out_copy(k, s):
        return pltpu.make_async_copy(out_buf.at[s], dst_ref.at[pl.ds(k * rows, rows), :], out_sem.at[s])

    @pl.when(step == 0)
    def _():
        in_copy(0, 0).start()

    in_copy(step, slot).wait()

    @pl.when(step + 1 < n_steps)
    def _():
        in_copy(step + 1, 1 - slot).start()

    @pl.when(step >= 2)
    def _():
        out_copy(step - 2, slot).wait()

    out_buf[slot] = in_buf[slot].astype(BF16)

    def finish():
        out_copy(step, slot).start()

        @pl.when(step == n_steps - 1)
        def _():
            if n_steps >= 2:
                out_copy(step - 1, 1 - slot).wait()
            out_copy(step, slot).wait()

    return finish


def _ffn_body(*refs, final_norm, sub, n_steps, convert_layer, pool, extra):
    it = iter(refs)
    take = lambda n: [next(it) for _ in range(n)]
    do_convert = convert_layer is not None
    x_ref, g_ref, wa_ref, wb_ref, wo_ref, gf_ref = take(6)
    (xs_ref,) = take(1) if extra else [None]
    hist_ref, gm_ref, pw_ref, pb_ref, ps_ref = take(5) if pool else [None] * 5
    nwi_ref, nwo_ref = take(2) if do_convert else [None] * 2
    (o_ref,) = take(1)
    (os_ref,) = take(1) if extra else [None]
    (hist_out_ref,) = take(1) if pool else [None]
    cwi_ref, cwo_ref = take(2) if do_convert else [None] * 2
    ext_ref, carry_ref = take(2) if pool else [None] * 2
    conv_scratch = take(8) if do_convert else None
    step = pl.program_id(0)
    tm = x_ref.shape[0]

    def ffn_rows(x):
        xn = _rmsnorm_f32(x, g_ref[...]).astype(BF16)
        a = jnp.dot(xn, wa_ref[...], preferred_element_type=F32)
        b = jnp.dot(xn, wb_ref[...], preferred_element_type=F32)
        h = (a * jax.nn.sigmoid(a) * b).astype(BF16)
        y = x + 0.5 * jnp.dot(h, wo_ref[...], preferred_element_type=F32)
        return _rmsnorm_f32(y, gf_ref[...]) if final_norm else y

    def main_step():
        if pool:
            tiles_per_seq, pos0 = pool
            t_in_seq = step % tiles_per_seq

            @pl.when(t_in_seq == 0)
            def _():
                carry_ref[...] = hist_ref[0]

        finishers = []
        if do_convert:
            wi_in, wi_out, wo_in, wo_out, sem_ii, sem_io, sem_oi, sem_oo = conv_scratch
            finishers = [
                _convert_slab(nwi_ref.at[convert_layer], cwi_ref, wi_in, wi_out, sem_ii, sem_io, step=step,
                              n_steps=n_steps),
                _convert_slab(nwo_ref.at[convert_layer], cwo_ref, wo_in, wo_out, sem_oi, sem_oo, step=step,
                              n_steps=n_steps),
            ]

        def pool_and_store(r0, y):
            y, tail = _pool_rows(y, carry_ref[...], ext_ref, pos0 + t_in_seq * tm + r0, gm_ref[...], pw_ref,
                                 pb_ref[...], ps_ref[...])
            carry_ref[...] = tail
            hist_out_ref[0] = tail
            o_ref[r0:r0 + sub, :] = y

        deferred = None
        for r0 in range(0, tm, sub):
            y = ffn_rows(x_ref[r0:r0 + sub, :])
            if pool:
                if deferred is not None:
                    pool_and_store(*deferred)
                deferred = (r0, y)
            else:
                o_ref[r0:r0 + sub, :] = y
        if deferred is not None:
            pool_and_store(*deferred)
        for finish in finishers:
            finish()

    if not extra:
        main_step()
        return
    pl.when(step < n_steps)(main_step)

    @pl.when(step == n_steps)
    def _():
        for r0 in range(0, xs_ref.shape[0], sub):
            os_ref[r0:r0 + sub, :] = ffn_rows(xs_ref[r0:r0 + sub, :])


def _ffn(x, g, w_in, w_out, g_final, *, layer, final_norm, tm, convert=None, pool=None, extra_x=None):
    T = x.shape[0]
    assert T % tm == 0
    n_steps = T // tm
    sub = min(tm, FFN_SUB_ROWS)
    last = n_steps - 1
    row = pl.BlockSpec((tm, D_MODEL), lambda i: (jnp.minimum(i, last), 0))
    in_specs = [
        row,
        _resident((1, D_MODEL), layer),
        pl.BlockSpec((D_MODEL, D_FF), lambda i: (0, 0), pipeline_mode=pl.Buffered(1)),
        pl.BlockSpec((D_MODEL, D_FF), lambda i: (0, 1), pipeline_mode=pl.Buffered(1)),
        pl.BlockSpec((D_FF, D_MODEL), lambda i: (0, 0), pipeline_mode=pl.Buffered(1)),
        _resident((1, D_MODEL), 0),
    ]
    out_specs = [row]
    out_shape = [jax.ShapeDtypeStruct((T, D_MODEL), F32)]
    operands = [x, g, w_in, w_in, w_out, g_final]
    names = ['y']
    scratch = []
    if extra_x is not None:
        Ts = extra_x.shape[0]
        assert Ts % sub == 0
        in_specs += [pl.BlockSpec((Ts, D_MODEL), lambda i: (0, 0), pipeline_mode=pl.Buffered(1))]
        out_specs += [pl.BlockSpec((Ts, D_MODEL), lambda i: (0, 0))]
        out_shape += [jax.ShapeDtypeStruct((Ts, D_MODEL), F32)]
        operands += [extra_x]
        names += ['y_extra']
    pool_cfg = None
    if pool is not None:
        hist, g_mix, pw, pb, psc, pool_layer, seq, pos0 = pool
        assert seq % tm == 0 and sub >= HIST_ROWS
        tiles_per_seq = seq // tm
        pool_cfg = (tiles_per_seq, pos0)
        in_specs += [
            pl.BlockSpec((None, 1, HIST_ROWS, D_MODEL),
                         lambda i: (pool_layer, jnp.minimum(i, last) // tiles_per_seq, 0, 0)),
            _resident((1, D_MODEL), layer),
            _resident(pw.shape[1:], pool_layer),
            _resident((1, D_MODEL), pool_layer),
            _resident((1, D_MODEL), pool_layer),
        ]
        out_specs += [pl.BlockSpec((1, HIST_ROWS, D_MODEL), lambda i: (jnp.minimum(i, last) // tiles_per_seq, 0, 0))]
        out_shape += [jax.ShapeDtypeStruct((T // seq, HIST_ROWS, D_MODEL), F32)]
        operands += [hist, g_mix, pw, pb, psc]
        names += ['hist']
        scratch += [pltpu.VMEM((HIST_ROWS + sub, D_MODEL), F32), pltpu.VMEM((HIST_ROWS, D_MODEL), F32)]
    convert_layer = None
    if convert is not None:
        src_in, src_out, convert_layer = convert
        assert D_MODEL % n_steps == 0 and D_FF % n_steps == 0
        r_in, r_out = D_MODEL // n_steps, D_FF // n_steps
        any_spec = pl.BlockSpec(memory_space=pl.ANY)
        in_specs += [any_spec, any_spec]
        out_specs += [any_spec, any_spec]
        out_shape += [jax.ShapeDtypeStruct((D_MODEL, 2 * D_FF), BF16), jax.ShapeDtypeStruct((D_FF, D_MODEL), BF16)]
        operands += [src_in, src_out]
        names += ['w_in', 'w_out']
        scratch += [
            pltpu.VMEM((2, r_in, 2 * D_FF), F32), pltpu.VMEM((2, r_in, 2 * D_FF), BF16),
            pltpu.VMEM((2, r_out, D_MODEL), F32), pltpu.VMEM((2, r_out, D_MODEL), BF16),
        ] + [pltpu.SemaphoreType.DMA((2,))] * 4
    outs = pl.pallas_call(
        functools.partial(_ffn_body, final_norm=final_norm, sub=sub, n_steps=n_steps,
                          convert_layer=convert_layer, pool=pool_cfg, extra=extra_x is not None),
        grid=(n_steps + (extra_x is not None),),
        in_specs=in_specs,
        out_specs=out_specs,
        out_shape=out_shape,
        scratch_shapes=scratch,
        compiler_params=pltpu.CompilerParams(
            dimension_semantics=("arbitrary",), vmem_limit_bytes=VMEM_LIMIT_BYTES),
        name="ffn",
    )(*operands)
    return dict(zip(names, outs))


def _pool_body(x_ref, hist_ref, g_ref, w_ref, b_ref, sc_ref, o_ref, hist_out_ref, ext_ref, carry_ref,
               *, ts, pos0):
    t = pl.program_id(1)

    @pl.when(t == 0)
    def _():
        carry_ref[...] = hist_ref[0]

    y, tail = _pool_rows(x_ref[0], carry_ref[...], ext_ref, pos0 + t * ts, g_ref[...], w_ref, b_ref[...],
                         sc_ref[...])
    o_ref[0] = y
    carry_ref[...] = tail
    hist_out_ref[0] = tail


def _pool_mixer(x, hist, g, w, b, sc, *, layer, pool_layer, pos0, ts):
    B, S, _ = x.shape
    assert S % ts == 0 and ts >= HIST_ROWS
    return pl.pallas_call(
        functools.partial(_pool_body, ts=ts, pos0=pos0),
        grid=(B, S // ts),
        in_specs=[
            pl.BlockSpec((1, ts, D_MODEL), lambda bi, t: (bi, t, 0)),
            pl.BlockSpec((None, 1, HIST_ROWS, D_MODEL), lambda bi, t: (pool_layer, bi, 0, 0)),
            _resident((1, D_MODEL), layer),
            _resident(w.shape[1:], pool_layer),
            _resident((1, D_MODEL), pool_layer),
            _resident((1, D_MODEL), pool_layer),
        ],
        out_specs=[
            pl.BlockSpec((1, ts, D_MODEL), lambda bi, t: (bi, t, 0)),
            pl.BlockSpec((1, HIST_ROWS, D_MODEL), lambda bi, t: (bi, 0, 0)),
        ],
        out_shape=[
            jax.ShapeDtypeStruct((B, S, D_MODEL), F32),
            jax.ShapeDtypeStruct((B, HIST_ROWS, D_MODEL), F32),
        ],
        scratch_shapes=[pltpu.VMEM((HIST_ROWS + ts, D_MODEL), F32), pltpu.VMEM((HIST_ROWS, D_MODEL), F32)],
        compiler_params=pltpu.CompilerParams(
            dimension_semantics=("arbitrary", "arbitrary"), vmem_limit_bytes=VMEM_LIMIT_BYTES),
        name="pool_mixer",
    )(x, hist, g, w, b, sc)


def _log_sigmoid(x):
    return jnp.minimum(x, 0.0) - jnp.log1p(jnp.exp(-jnp.abs(x)))


def _dot_nt(a, b):
    return lax.dot_general(a, b, (((1,), (1,)), ((), ())), preferred_element_type=F32)


def _dot_tn(a, b):
    return lax.dot_general(a, b, (((0,), (0,)), ((), ())), preferred_element_type=F32)


def _mlstm_body(x_ref, c0_ref, n0_ref, m0_ref, g_ref, win_ref, wg_ref, bg_ref, gn_ref, wo_ref,
                o_ref, c_ref, n_ref, m_ref, *scratch, tl, L, per_chunk_state, sub):
    H = MLSTM_HEADS
    t = pl.program_id(1)
    n_sub = tl // sub
    p_ss, gate_ss, hn_ss = scratch[:n_sub], scratch[n_sub:2 * n_sub], scratch[2 * n_sub:]

    @pl.when(t == 0)
    def _():
        c_ref[...] = c0_ref[...]
        n_ref[...] = n0_ref[...]
        m_ref[...] = m0_ref[...]

    for i_sub in range(n_sub):
        x = x_ref[i_sub * sub:(i_sub + 1) * sub, :]
        u = _rmsnorm_f32(x, g_ref[...]).astype(BF16)
        p_ss[i_sub][...] = jnp.dot(u, win_ref[...], preferred_element_type=F32)
        pre = jnp.dot(u, wg_ref[...], preferred_element_type=F32) + bg_ref[...]
        lane = lax.broadcasted_iota(jnp.int32, pre.shape, 1)
        gate_ss[i_sub][...] = jnp.where(lane < H, pre, _log_sigmoid(pre))

    row_g = lax.broadcasted_iota(jnp.int32, (L, GATE_LANES), 0)
    causal = lax.broadcasted_iota(jnp.int32, (L, L), 0) >= lax.broadcasted_iota(jnp.int32, (L, L), 1)
    for c in range(tl // L):
        i_sub, r0 = divmod(c * L, sub)
        p_s, gate_s, hn_s = p_ss[i_sub], gate_ss[i_sub], hn_ss[i_sub]
        si = c if per_chunk_state else 0
        G = gate_s[r0:r0 + L, :]
        Bc = G
        k = 1
        while k < L:
            Bc = Bc + jnp.where(row_g >= k, pltpu.roll(Bc, k, 0), 0.0)
            k *= 2
        A = G - pltpu.roll(Bc, GATE_LANES - H, 1)
        AT = A.T
        for h in range(H):
            bcol = Bc[:, H + h:H + h + 1]
            acol = A[:, h:h + 1]
            arow = AT[h:h + 1, :]
            C = c_ref[si, h]
            n = n_ref[si, h]
            m_prev = m_ref[si, h][:, 0:1]
            q = p_s[r0:r0 + L, h * MLSTM_DQK:(h + 1) * MLSTM_DQK] * (MLSTM_DQK ** -0.5)
            kk = p_s[r0:r0 + L, MLSTM_QK + h * MLSTM_DQK:MLSTM_QK + (h + 1) * MLSTM_DQK]
            v0 = 2 * MLSTM_QK + h * MLSTM_DV
            v = p_s[r0:r0 + L, v0:v0 + MLSTM_DV]
            og = p_s[r0:r0 + L, v0 + MLSTM_V:v0 + MLSTM_V + MLSTM_DV]
            qb = q.astype(BF16)
            kb = kk.astype(BF16)

            log_d = jnp.where(causal, bcol + arow, -jnp.inf)
            log_state = bcol + m_prev
            m_t = jnp.maximum(jnp.max(log_d, axis=-1, keepdims=True), log_state)
            w_intra = jnp.exp(log_d - m_t)
            w_state = jnp.exp(log_state - m_t)
            s = _dot_nt(qb, kb) * w_intra
            num = (jnp.dot(s.astype(BF16), v.astype(BF16), preferred_element_type=F32)
                   + w_state * _dot_nt(qb, C.astype(BF16)))
            den = jnp.sum(s, axis=-1, keepdims=True) + w_state * jnp.sum(q * n, axis=-1, keepdims=True)
            hh = num / jnp.maximum(jnp.abs(den), jnp.exp(-m_t))
            hh = hh * lax.rsqrt(jnp.mean(hh * hh, axis=-1, keepdims=True) + EPS)
            hh = hh * gn_ref[:, h * MLSTM_DV:(h + 1) * MLSTM_DV]
            hn_s[r0:r0 + L, h * MLSTM_DV:(h + 1) * MLSTM_DV] = (jax.nn.sigmoid(og) * hh).astype(BF16)

            m_new = m_t[L - 1:L, :]
            b_last = bcol[L - 1:L, :]
            w_carry = jnp.exp(b_last + m_prev - m_new)
            w_write = jnp.exp(b_last + acol - m_new)
            c_ref[si, h] = w_carry * C + _dot_tn((v * w_write).astype(BF16), kb)
            n_ref[si, h] = w_carry * n + jnp.sum(kk * w_write, axis=0, keepdims=True)
            m_ref[si, h] = jnp.broadcast_to(m_new, (1, GATE_LANES))

        if r0 + L == sub:
            rows = slice(i_sub * sub, (i_sub + 1) * sub)
            o_ref[rows, :] = x_ref[rows, :] + jnp.dot(hn_s[...], wo_ref[...], preferred_element_type=F32)


def _mlstm_mixer(x, C0, n0, m0, g, w_main, w_gate, b_gate, g_norm, w_out,
                 *, layer, cell_layer, seq, tl, L, per_chunk_state):
    T = x.shape[0]
    H = MLSTM_HEADS
    B = C0.shape[1]
    jl = cell_layer
    sub = min(tl, MLSTM_SUB_ROWS)
    n_sub = tl // sub
    nb = B if per_chunk_state else 1
    tiles_per_seq = 1 if per_chunk_state else seq // tl
    if per_chunk_state:
        assert seq == L and tl == nb * L and T == tl
    else:
        assert seq % tl == 0 and T == B * seq
    assert tl % sub == 0 and sub % L == 0
    row = pl.BlockSpec((tl, D_MODEL), lambda b, t: (b * tiles_per_seq + t, 0))
    c_spec = pl.BlockSpec((nb, H, MLSTM_DV, MLSTM_DQK), lambda b, t: (b, 0, 0, 0))
    n_spec = pl.BlockSpec((nb, H, 1, MLSTM_DQK), lambda b, t: (b, 0, 0, 0))
    m_spec = pl.BlockSpec((nb, H, 1, GATE_LANES), lambda b, t: (b, 0, 0, 0))
    return pl.pallas_call(
        functools.partial(_mlstm_body, tl=tl, L=L, per_chunk_state=per_chunk_state, sub=sub),
        grid=(B // nb, tiles_per_seq),
        in_specs=[
            row,
            pl.BlockSpec((None, nb, H, MLSTM_DV, MLSTM_DQK), lambda b, t: (jl, b, 0, 0, 0)),
            pl.BlockSpec((None, nb, H, 1, MLSTM_DQK), lambda b, t: (jl, b, 0, 0, 0)),
            pl.BlockSpec((None, nb, H, 1, GATE_LANES), lambda b, t: (jl, b, 0, 0, 0)),
            _resident((1, D_MODEL), layer),
            _resident((D_MODEL, MLSTM_MAIN), jl),
            _resident((D_MODEL, GATE_LANES), jl),
            _resident((1, GATE_LANES), jl),
            _resident((1, MLSTM_V), jl),
            _resident((MLSTM_V, D_MODEL), jl),
        ],
        out_specs=[row, c_spec, n_spec, m_spec],
        out_shape=[
            jax.ShapeDtypeStruct((T, D_MODEL), F32),
            jax.ShapeDtypeStruct((B, H, MLSTM_DV, MLSTM_DQK), F32),
            jax.ShapeDtypeStruct((B, H, 1, MLSTM_DQK), F32),
            jax.ShapeDtypeStruct((B, H, 1, GATE_LANES), F32),
        ],
        scratch_shapes=([pltpu.VMEM((sub, MLSTM_MAIN), F32)] * n_sub
                        + [pltpu.VMEM((sub, GATE_LANES), F32)] * n_sub
                        + [pltpu.VMEM((sub, MLSTM_V), BF16)] * n_sub),
        compiler_params=pltpu.CompilerParams(
            dimension_semantics=("arbitrary", "arbitrary"), vmem_limit_bytes=VMEM_LIMIT_BYTES),
        name="mlstm_layer",
    )(x, C0, n0, m0, g, w_main, w_gate, b_gate, g_norm, w_out)


def _forward(xp, xs, state_pool, mC_s, mn_s, mm_s, W):
    Bp, Sp, _ = xp.shape
    Bs, Ss, _ = xs.shape
    n_pool, n_cell = state_pool.shape[0], mC_s.shape[0]
    hist_p = jnp.zeros((n_pool, Bp, HIST_ROWS, D_MODEL), F32)
    hist_s = jnp.pad(state_pool, ((0, 0), (0, 0), (1, 0), (0, 0)))
    mC_p = jnp.zeros((n_cell, Bp) + mC_s.shape[2:], F32)
    n0_p = jnp.zeros((n_cell, Bp, MLSTM_HEADS, 1, MLSTM_DQK), F32)
    m0_p = jnp.zeros((n_cell, Bp, MLSTM_HEADS, 1, GATE_LANES), F32)
    n0_s = mn_s[:, :, :, None, :]
    m0_s = jnp.broadcast_to(mm_s[:, :, :, None, None], mm_s.shape + (1, GATE_LANES))

    ffn_calls = [(name, i) for i in range(DEPTH) for name in ('ffn1', 'ffn2')]
    w_cur = (W['ffn1_w_in'][0].astype(BF16), W['ffn1_w_out'][0].astype(BF16))

    def run_ffn(k, xp2, xs2, pool=None):
        nonlocal w_cur
        name, layer = ffn_calls[k]
        cv = None
        if k + 1 < len(ffn_calls):
            nxt, nxt_layer = ffn_calls[k + 1]
            cv = (W[nxt + '_w_in'], W[nxt + '_w_out'], nxt_layer)
        out = _ffn(xp2, W['norm_' + name], w_cur[0], w_cur[1], W['norm_final'], layer=layer,
                   final_norm=(k == len(ffn_calls) - 1), tm=FFN_TILE_ROWS, convert=cv, pool=pool, extra_x=xs2)
        if cv is not None:
            w_cur = (out['w_in'], out['w_out'])
        return out['y'], out['y_extra'], out.get('hist')

    out_p = dict(pool=[], C=[], n=[], m=[])
    out_s = dict(pool=[], C=[], n=[], m=[])
    xp2 = xp.reshape(Bp * Sp, D_MODEL)
    xs2 = xs.reshape(Bs * Ss, D_MODEL)
    for i in range(DEPTH):
        j = i // 2
        if i % 2 == 0:
            xp2, xs2, hist_new = run_ffn(2 * i, xp2, xs2, pool=(hist_p, W['norm_mix'], W['pool_w'], W['pool_b'],
                                                                W['pool_scale'], j, Sp, 0))
            out_p['pool'].append(hist_new[:, 1:])
            xs3, hist_new = _pool_mixer(xs2.reshape(Bs, Ss, D_MODEL), hist_s, W['norm_mix'], W['pool_w'],
                                        W['pool_b'], W['pool_scale'], layer=i, pool_layer=j, pos0=PAST_LEN, ts=Ss)
            xs2 = xs3.reshape(Bs * Ss, D_MODEL)
            out_s['pool'].append(hist_new[:, 1:])
        else:
            xp2, xs2, _ = run_ffn(2 * i, xp2, xs2)
            cell_w = (W['norm_mix'], W['mlstm_w_in'], W['mlstm_w_gate'], W['mlstm_b_gate'], W['mlstm_norm'],
                      W['mlstm_w_out'])
            xp2, C, n, m = _mlstm_mixer(xp2, mC_p, n0_p, m0_p, *cell_w, layer=i, cell_layer=j, seq=Sp,
                                        tl=MLSTM_TILE_ROWS, L=MLSTM_CHUNK, per_chunk_state=False)
            for key, val in (('C', C), ('n', n[:, :, 0, :]), ('m', m[:, :, 0, 0])):
                out_p[key].append(val)
            xs2, C, n, m = _mlstm_mixer(xs2, mC_s, n0_s, m0_s, *cell_w, layer=i, cell_layer=j, seq=Ss,
                                        tl=Bs * Ss, L=Ss, per_chunk_state=True)
            for key, val in (('C', C), ('n', n[:, :, 0, :]), ('m', m[:, :, 0, 0])):
                out_s[key].append(val)
        xp2, xs2, _ = run_ffn(2 * i + 1, xp2, xs2)
    stack = lambda d: tuple(jnp.stack(d[key]) for key in ('pool', 'C', 'n', 'm'))
    return (xp2.reshape(Bp, Sp, D_MODEL), xs2.reshape(Bs, Ss, D_MODEL)) + stack(out_p) + stack(out_s)


def kernel(x_prompt, x_sample, state_pool, state_mlstm_C, state_mlstm_n, state_mlstm_m, norm_ffn1,
           ffn1_w_in, ffn1_w_out, norm_mix, pool_w, pool_b, pool_scale, mlstm_w_in, mlstm_b_i, mlstm_b_f,
           mlstm_norm, mlstm_w_out, norm_ffn2, ffn2_w_in, ffn2_w_out, norm_final):
    n_gate = 2 * MLSTM_HEADS
    W = dict(
        norm_ffn1=norm_ffn1[:, None, :], norm_ffn2=norm_ffn2[:, None, :], norm_mix=norm_mix[:, None, :],
        norm_final=norm_final[None, None, :],
        ffn1_w_in=ffn1_w_in, ffn1_w_out=ffn1_w_out, ffn2_w_in=ffn2_w_in, ffn2_w_out=ffn2_w_out,
        pool_w=pool_w.astype(BF16), pool_b=pool_b[:, None, :], pool_scale=pool_scale[:, None, :],
        mlstm_w_in=mlstm_w_in.astype(BF16),
        mlstm_w_gate=jnp.pad(mlstm_w_in[:, :, MLSTM_MAIN:], ((0, 0), (0, 0), (0, GATE_LANES - n_gate))).astype(BF16),
        mlstm_b_gate=jnp.pad(jnp.concatenate([mlstm_b_i, mlstm_b_f], axis=-1),
                             ((0, 0), (0, GATE_LANES - n_gate)))[:, None, :],
        mlstm_norm=mlstm_norm[:, None, :], mlstm_w_out=mlstm_w_out.astype(BF16),
    )
    return _forward(x_prompt, x_sample, state_pool, state_mlstm_C, state_mlstm_n, state_mlstm_m, W)
```

```python
import functools

import jax
import jax.numpy as jnp
from jax import lax
from jax.experimental import pallas as pl
from jax.experimental.pallas import tpu as pltpu

D_MODEL = 1024
D_FF = 2816
DEPTH = 4
PAST_LEN = 1024
POOL_WINDOWS = (2, 4, 8, 16)
POOL_GROUP_WIDTH = D_MODEL // len(POOL_WINDOWS)
POOL_HIST = max(POOL_WINDOWS) - 1
HIST_ROWS = POOL_HIST + 1
MLSTM_HEADS = 4
MLSTM_DQK = D_MODEL // (2 * MLSTM_HEADS)
MLSTM_DV = D_MODEL // MLSTM_HEADS
MLSTM_QK = MLSTM_HEADS * MLSTM_DQK
MLSTM_V = MLSTM_HEADS * MLSTM_DV
MLSTM_MAIN = 2 * MLSTM_QK + 2 * MLSTM_V
GATE_LANES = 128
EPS = 1e-6

VMEM_LIMIT_BYTES = 56 * 1024 * 1024
FFN_SUB_ROWS = 256
MLSTM_SUB_ROWS = 512
FFN_TILE_ROWS = 1024
MLSTM_TILE_ROWS = 1024
MLSTM_CHUNK = 256

F32 = jnp.float32
BF16 = jnp.bfloat16


def _rmsnorm_f32(x, g):
    return x * lax.rsqrt(jnp.mean(x * x, axis=-1, keepdims=True) + EPS) * g


def _resident(shape, layer, col=0):
    idx = (layer,) + (0,) * (len(shape) - 1) + (col,)
    return pl.BlockSpec((None,) + tuple(shape), lambda *_: idx, pipeline_mode=pl.Buffered(1))


def _pool_rows(x, head, ext_ref, pos_first, g, w_ref, b, sc):
    n = x.shape[0]
    u = _rmsnorm_f32(x, g)
    ext_ref[0:HIST_ROWS, :] = head
    ext_ref[HIST_ROWS:HIST_ROWS + n, :] = u
    ext = ext_ref[...]
    pos = pos_first + lax.broadcasted_iota(jnp.int32, (n, 1), 0)
    ys = []
    for gi, w in enumerate(POOL_WINDOWS):
        c0 = gi * POOL_GROUP_WIDTH
        s = ext[:, c0:c0 + POOL_GROUP_WIDTH]
        k = 1
        while k < w:
            s = s + pltpu.roll(s, k, 0)
            k *= 2
        cnt = jnp.minimum(pos + 1, w).astype(F32)
        d = s[HIST_ROWS:, :] / cnt - u[:, c0:c0 + POOL_GROUP_WIDTH]
        ys.append(jnp.dot(d.astype(BF16), w_ref[gi], preferred_element_type=F32))
    y = (jnp.concatenate(ys, axis=-1) + b) * sc
    return x + y, ext_ref[n:n + HIST_ROWS, :]


def _convert_slab(src_ref, dst_ref, in_buf, out_buf, in_sem, out_sem, *, step, n_steps):
    rows = in_buf.shape[1]
    slot = step % 2

    def in_copy(k, s):
        return pltpu.make_async_copy(src_ref.at[pl.ds(k * rows, rows), :], in_buf.at[s], in_sem.at[s])

    def out_copy(k, s):
        return pltpu.make_async_copy(out_buf.at[s], dst_ref.at[pl.ds(k * rows, rows), :], out_sem.at[s])

    @pl.when(step == 0)
    def _():
        in_copy(0, 0).start()

    in_copy(step, slot).wait()

    @pl.when(step + 1 < n_steps)
    def _():
        in_copy(step + 1, 1 - slot).start()

    @pl.when(step >= 2)
    def _():
        out_copy(step - 2, slot).wait()

    out_buf[slot] = in_buf[slot].astype(BF16)

    def finish():
        out_copy(step, slot).start()

        @pl.when(step == n_steps - 1)
        def _():
            if n_steps >= 2:
                out_copy(step - 1, 1 - slot).wait()
            out_copy(step, slot).wait()

    return finish


def _ffn_body(*refs, final_norm, sub, n_steps, convert_layer, pool, extra):
    it = iter(refs)
    take = lambda n: [next(it) for _ in range(n)]
    do_convert = convert_layer is not None
    x_ref, g_ref, wa_ref, wb_ref, wo_ref, gf_ref = take(6)
    (xs_ref,) = take(1) if extra else [None]
    hist_ref, gm_ref, pw_ref, pb_ref, ps_ref = take(5) if pool else [None] * 5
    nwi_ref, nwo_ref = take(2) if do_convert else [None] * 2
    (o_ref,) = take(1)
    (os_ref,) = take(1) if extra else [None]
    (hist_out_ref,) = take(1) if pool else [None]
    cwi_ref, cwo_ref = take(2) if do_convert else [None] * 2
    ext_ref, carry_ref = take(2) if pool else [None] * 2
    conv_scratch = take(8) if do_convert else None
    step = pl.program_id(0)
    tm = x_ref.shape[0]

    def ffn_rows(x):
        xn = _rmsnorm_f32(x, g_ref[...]).astype(BF16)
        a = jnp.dot(xn, wa_ref[...], preferred_element_type=F32)
        b = jnp.dot(xn, wb_ref[...], preferred_element_type=F32)
        h = (a * jax.nn.sigmoid(a) * b).astype(BF16)
        y = x + 0.5 * jnp.dot(h, wo_ref[...], preferred_element_type=F32)
        return _rmsnorm_f32(y, gf_ref[...]) if final_norm else y

    def main_step():
        if pool:
            tiles_per_seq, pos0 = pool
            t_in_seq = step % tiles_per_seq

            @pl.when(t_in_seq == 0)
            def _():
                carry_ref[...] = hist_ref[0]

        finishers = []
        if do_convert:
            wi_in, wi_out, wo_in, wo_out, sem_ii, sem_io, sem_oi, sem_oo = conv_scratch
            finishers = [
                _convert_slab(nwi_ref.at[convert_layer], cwi_ref, wi_in, wi_out, sem_ii, sem_io, step=step,
                              n_steps=n_steps),
                _convert_slab(nwo_ref.at[convert_layer], cwo_ref, wo_in, wo_out, sem_oi, sem_oo, step=step,
                              n_steps=n_steps),
            ]

        def pool_and_store(r0, y):
            y, tail = _pool_rows(y, carry_ref[...], ext_ref, pos0 + t_in_seq * tm + r0, gm_ref[...], pw_ref,
                                 pb_ref[...], ps_ref[...])
            carry_ref[...] = tail
            hist_out_ref[0] = tail
            o_ref[r0:r0 + sub, :] = y

        deferred = None
        for r0 in range(0, tm, sub):
            y = ffn_rows(x_ref[r0:r0 + sub, :])
            if pool:
                if deferred is not None:
                    pool_and_store(*deferred)
                deferred = (r0, y)
            else:
                o_ref[r0:r0 + sub, :] = y
        if deferred is not None:
            pool_and_store(*deferred)
        for finish in finishers:
            finish()

    if not extra:
        main_step()
        return
    pl.when(step < n_steps)(main_step)

    @pl.when(step == n_steps)
    def _():
        for r0 in range(0, xs_ref.shape[0], sub):
            os_ref[r0:r0 + sub, :] = ffn_rows(xs_ref[r0:r0 + sub, :])


def _ffn(x, g, w_in, w_out, g_final, *, layer, final_norm, tm, convert=None, pool=None, extra_x=None,
         in_place=False):
    T = x.shape[0]
    assert T % tm == 0
    n_steps = T // tm
    sub = min(tm, FFN_SUB_ROWS)
    last = n_steps - 1
    row = pl.BlockSpec((tm, D_MODEL), lambda i: (jnp.minimum(i, last), 0))
    in_specs = [
        row,
        _resident((1, D_MODEL), layer),
        pl.BlockSpec((D_MODEL, D_FF), lambda i: (0, 0), pipeline_mode=pl.Buffered(1)),
        pl.BlockSpec((D_MODEL, D_FF), lambda i: (0, 1), pipeline_mode=pl.Buffered(1)),
        pl.BlockSpec((D_FF, D_MODEL), lambda i: (0, 0), pipeline_mode=pl.Buffered(1)),
        _resident((1, D_MODEL), 0),
    ]
    out_specs = [row]
    out_shape = [jax.ShapeDtypeStruct((T, D_MODEL), F32)]
    operands = [x, g, w_in, w_in, w_out, g_final]
    names = ['y']
    scratch = []
    if extra_x is not None:
        Ts = extra_x.shape[0]
        assert Ts % sub == 0
        in_specs += [pl.BlockSpec((Ts, D_MODEL), lambda i: (0, 0), pipeline_mode=pl.Buffered(1))]
        out_specs += [pl.BlockSpec((Ts, D_MODEL), lambda i: (0, 0))]
        out_shape += [jax.ShapeDtypeStruct((Ts, D_MODEL), F32)]
        operands += [extra_x]
        names += ['y_extra']
    pool_cfg = None
    if pool is not None:
        hist, g_mix, pw, pb, psc, pool_layer, seq, pos0 = pool
        assert seq % tm == 0 and sub >= HIST_ROWS
        tiles_per_seq = seq // tm
        pool_cfg = (tiles_per_seq, pos0)
        in_specs += [
            pl.BlockSpec((None, 1, HIST_ROWS, D_MODEL),
                         lambda i: (pool_layer, jnp.minimum(i, last) // tiles_per_seq, 0, 0)),
            _resident((1, D_MODEL), layer),
            _resident(pw.shape[1:], pool_layer),
            _resident((1, D_MODEL), pool_layer),
            _resident((1, D_MODEL), pool_layer),
        ]
        out_specs += [pl.BlockSpec((1, HIST_ROWS, D_MODEL), lambda i: (jnp.minimum(i, last) // tiles_per_seq, 0, 0))]
        out_shape += [jax.ShapeDtypeStruct((T // seq, HIST_ROWS, D_MODEL), F32)]
        operands += [hist, g_mix, pw, pb, psc]
        names += ['hist']
        scratch += [pltpu.VMEM((HIST_ROWS + sub, D_MODEL), F32), pltpu.VMEM((HIST_ROWS, D_MODEL), F32)]
    convert_layer = None
    if convert is not None:
        src_in, src_out, convert_layer = convert
        assert D_MODEL % n_steps == 0 and D_FF % n_steps == 0
        r_in, r_out = D_MODEL // n_steps, D_FF // n_steps
        any_spec = pl.BlockSpec(memory_space=pl.ANY)
        in_specs += [any_spec, any_spec]
        out_specs += [any_spec, any_spec]
        out_shape += [jax.ShapeDtypeStruct((D_MODEL, 2 * D_FF), BF16), jax.ShapeDtypeStruct((D_FF, D_MODEL), BF16)]
        operands += [src_in, src_out]
        names += ['w_in', 'w_out']
        scratch += [
            pltpu.VMEM((2, r_in, 2 * D_FF), F32), pltpu.VMEM((2, r_in, 2 * D_FF), BF16),
            pltpu.VMEM((2, r_out, D_MODEL), F32), pltpu.VMEM((2, r_out, D_MODEL), BF16),
        ] + [pltpu.SemaphoreType.DMA((2,))] * 4
    outs = pl.pallas_call(
        functools.partial(_ffn_body, final_norm=final_norm, sub=sub, n_steps=n_steps,
                          convert_layer=convert_layer, pool=pool_cfg, extra=extra_x is not None),
        grid=(n_steps + (extra_x is not None),),
        in_specs=in_specs,
        out_specs=out_specs,
        out_shape=out_shape,
        scratch_shapes=scratch,
        input_output_aliases=({0: 0, 6: 1} if extra_x is not None else {0: 0}) if in_place else {},
        compiler_params=pltpu.CompilerParams(
            dimension_semantics=("arbitrary",), vmem_limit_bytes=VMEM_LIMIT_BYTES),
        name="ffn",
    )(*operands)
    return dict(zip(names, outs))


def _pool_body(x_ref, hist_ref, g_ref, w_ref, b_ref, sc_ref, o_ref, hist_out_ref, ext_ref, carry_ref,
               *, ts, pos0):
    t = pl.program_id(1)

    @pl.when(t == 0)
    def _():
        carry_ref[...] = hist_ref[0]

    y, tail = _pool_rows(x_ref[0], carry_ref[...], ext_ref, pos0 + t * ts, g_ref[...], w_ref, b_ref[...],
                         sc_ref[...])
    o_ref[0] = y
    carry_ref[...] = tail
    hist_out_ref[0] = tail


def _pool_mixer(x, hist, g, w, b, sc, *, layer, pool_layer, pos0, ts):
    B, S, _ = x.shape
    assert S % ts == 0 and ts >= HIST_ROWS
    return pl.pallas_call(
        functools.partial(_pool_body, ts=ts, pos0=pos0),
        grid=(B, S // ts),
        in_specs=[
            pl.BlockSpec((1, ts, D_MODEL), lambda bi, t: (bi, t, 0)),
            pl.BlockSpec((None, 1, HIST_ROWS, D_MODEL), lambda bi, t: (pool_layer, bi, 0, 0)),
            _resident((1, D_MODEL), layer),
            _resident(w.shape[1:], pool_layer),
            _resident((1, D_MODEL), pool_layer),
            _resident((1, D_MODEL), pool_layer),
        ],
        out_specs=[
            pl.BlockSpec((1, ts, D_MODEL), lambda bi, t: (bi, t, 0)),
            pl.BlockSpec((1, HIST_ROWS, D_MODEL), lambda bi, t: (bi, 0, 0)),
        ],
        out_shape=[
            jax.ShapeDtypeStruct((B, S, D_MODEL), F32),
            jax.ShapeDtypeStruct((B, HIST_ROWS, D_MODEL), F32),
        ],
        scratch_shapes=[pltpu.VMEM((HIST_ROWS + ts, D_MODEL), F32), pltpu.VMEM((HIST_ROWS, D_MODEL), F32)],
        compiler_params=pltpu.CompilerParams(
            dimension_semantics=("arbitrary", "arbitrary"), vmem_limit_bytes=VMEM_LIMIT_BYTES),
        name="pool_mixer",
    )(x, hist, g, w, b, sc)


def _log_sigmoid(x):
    return jnp.minimum(x, 0.0) - jnp.log1p(jnp.exp(-jnp.abs(x)))


def _dot_nt(a, b):
    return lax.dot_general(a, b, (((1,), (1,)), ((), ())), preferred_element_type=F32)


def _dot_tn(a, b):
    return lax.dot_general(a, b, (((0,), (0,)), ((), ())), preferred_element_type=F32)


def _mlstm_body(x_ref, c0_ref, n0_ref, m0_ref, g_ref, win_ref, wg_ref, bg_ref, gn_ref, wo_ref,
                o_ref, c_ref, n_ref, m_ref, *scratch, tl, L, per_chunk_state, sub):
    H = MLSTM_HEADS
    t = pl.program_id(1)
    n_sub = tl // sub
    p_ss, gate_ss, hn_ss = scratch[:n_sub], scratch[n_sub:2 * n_sub], scratch[2 * n_sub:]

    @pl.when(t == 0)
    def _():
        c_ref[...] = c0_ref[...]
        n_ref[...] = n0_ref[...]
        m_ref[...] = m0_ref[...]

    for i_sub in range(n_sub):
        x = x_ref[i_sub * sub:(i_sub + 1) * sub, :]
        u = _rmsnorm_f32(x, g_ref[...]).astype(BF16)
        p_ss[i_sub][...] = jnp.dot(u, win_ref[...], preferred_element_type=F32)
        pre = jnp.dot(u, wg_ref[...], preferred_element_type=F32) + bg_ref[...]
        lane = lax.broadcasted_iota(jnp.int32, pre.shape, 1)
        gate_ss[i_sub][...] = jnp.where(lane < H, pre, _log_sigmoid(pre))

    row_g = lax.broadcasted_iota(jnp.int32, (L, GATE_LANES), 0)
    causal = lax.broadcasted_iota(jnp.int32, (L, L), 0) >= lax.broadcasted_iota(jnp.int32, (L, L), 1)
    for c in range(tl // L):
        i_sub, r0 = divmod(c * L, sub)
        p_s, gate_s, hn_s = p_ss[i_sub], gate_ss[i_sub], hn_ss[i_sub]
        si = c if per_chunk_state else 0
        G = gate_s[r0:r0 + L, :]
        Bc = G
        k = 1
        while k < L:
            Bc = Bc + jnp.where(row_g >= k, pltpu.roll(Bc, k, 0), 0.0)
            k *= 2
        A = G - pltpu.roll(Bc, GATE_LANES - H, 1)
        AT = A.T
        for h in range(H):
            bcol = Bc[:, H + h:H + h + 1]
            acol = A[:, h:h + 1]
            arow = AT[h:h + 1, :]
            C = c_ref[si, h]
            n = n_ref[si, h]
            m_prev = m_ref[si, h][:, 0:1]
            q = p_s[r0:r0 + L, h * MLSTM_DQK:(h + 1) * MLSTM_DQK] * (MLSTM_DQK ** -0.5)
            kk = p_s[r0:r0 + L, MLSTM_QK + h * MLSTM_DQK:MLSTM_QK + (h + 1) * MLSTM_DQK]
            v0 = 2 * MLSTM_QK + h * MLSTM_DV
            v = p_s[r0:r0 + L, v0:v0 + MLSTM_DV]
            og = p_s[r0:r0 + L, v0 + MLSTM_V:v0 + MLSTM_V + MLSTM_DV]
            qb = q.astype(BF16)
            kb = kk.astype(BF16)

            log_d = jnp.where(causal, bcol + arow, -jnp.inf)
            log_state = bcol + m_prev
            m_t = jnp.maximum(jnp.max(log_d, axis=-1, keepdims=True), log_state)
            w_intra = jnp.exp(log_d - m_t)
            w_state = jnp.exp(log_state - m_t)
            s = _dot_nt(qb, kb) * w_intra
            num = (jnp.dot(s.astype(BF16), v.astype(BF16), preferred_element_type=F32)
                   + w_state * _dot_nt(qb, C.astype(BF16)))
            den = jnp.sum(s, axis=-1, keepdims=True) + w_state * jnp.sum(q * n, axis=-1, keepdims=True)
            hh = num / jnp.maximum(jnp.abs(den), jnp.exp(-m_t))
            hh = hh * lax.rsqrt(jnp.mean(hh * hh, axis=-1, keepdims=True) + EPS)
            hh = hh * gn_ref[:, h * MLSTM_DV:(h + 1) * MLSTM_DV]
            hn_s[r0:r0 + L, h * MLSTM_DV:(h + 1) * MLSTM_DV] = (jax.nn.sigmoid(og) * hh).astype(BF16)

            m_new = m_t[L - 1:L, :]
            b_last = bcol[L - 1:L, :]
            w_carry = jnp.exp(b_last + m_prev - m_new)
            w_write = jnp.exp(b_last + acol - m_new)
            c_ref[si, h] = w_carry * C + _dot_tn((v * w_write).astype(BF16), kb)
            n_ref[si, h] = w_carry * n + jnp.sum(kk * w_write, axis=0, keepdims=True)
            m_ref[si, h] = jnp.broadcast_to(m_new, (1, GATE_LANES))

        if r0 + L == sub:
            rows = slice(i_sub * sub, (i_sub + 1) * sub)
            o_ref[rows, :] = x_ref[rows, :] + jnp.dot(hn_s[...], wo_ref[...], preferred_element_type=F32)


def _mlstm_mixer(x, C0, n0, m0, g, w_main, w_gate, b_gate, g_norm, w_out,
                 *, layer, cell_layer, seq, tl, L, per_chunk_state):
    T = x.shape[0]
    H = MLSTM_HEADS
    B = C0.shape[1]
    jl = cell_layer
    sub = min(tl, MLSTM_SUB_ROWS)
    n_sub = tl // sub
    nb = B if per_chunk_state else 1
    tiles_per_seq = 1 if per_chunk_state else seq // tl
    if per_chunk_state:
        assert seq == L and tl == nb * L and T == tl
    else:
        assert seq % tl == 0 and T == B * seq
    assert tl % sub == 0 and sub % L == 0
    row = pl.BlockSpec((tl, D_MODEL), lambda b, t: (b * tiles_per_seq + t, 0))
    c_spec = pl.BlockSpec((nb, H, MLSTM_DV, MLSTM_DQK), lambda b, t: (b, 0, 0, 0))
    n_spec = pl.BlockSpec((nb, H, 1, MLSTM_DQK), lambda b, t: (b, 0, 0, 0))
    m_spec = pl.BlockSpec((nb, H, 1, GATE_LANES), lambda b, t: (b, 0, 0, 0))
    return pl.pallas_call(
        functools.partial(_mlstm_body, tl=tl, L=L, per_chunk_state=per_chunk_state, sub=sub),
        grid=(B // nb, tiles_per_seq),
        in_specs=[
            row,
            pl.BlockSpec((None, nb, H, MLSTM_DV, MLSTM_DQK), lambda b, t: (jl, b, 0, 0, 0)),
            pl.BlockSpec((None, nb, H, 1, MLSTM_DQK), lambda b, t: (jl, b, 0, 0, 0)),
            pl.BlockSpec((None, nb, H, 1, GATE_LANES), lambda b, t: (jl, b, 0, 0, 0)),
            _resident((1, D_MODEL), layer),
            _resident((D_MODEL, MLSTM_MAIN), jl),
            _resident((D_MODEL, GATE_LANES), jl),
            _resident((1, GATE_LANES), jl),
            _resident((1, MLSTM_V), jl),
            _resident((MLSTM_V, D_MODEL), jl),
        ],
        out_specs=[row, c_spec, n_spec, m_spec],
        out_shape=[
            jax.ShapeDtypeStruct((T, D_MODEL), F32),
            jax.ShapeDtypeStruct((B, H, MLSTM_DV, MLSTM_DQK), F32),
            jax.ShapeDtypeStruct((B, H, 1, MLSTM_DQK), F32),
            jax.ShapeDtypeStruct((B, H, 1, GATE_LANES), F32),
        ],
        scratch_shapes=([pltpu.VMEM((sub, MLSTM_MAIN), F32)] * n_sub
                        + [pltpu.VMEM((sub, GATE_LANES), F32)] * n_sub
                        + [pltpu.VMEM((sub, MLSTM_V), BF16)] * n_sub),
        compiler_params=pltpu.CompilerParams(
            dimension_semantics=("arbitrary", "arbitrary"), vmem_limit_bytes=VMEM_LIMIT_BYTES),
        name="mlstm_layer",
    )(x, C0, n0, m0, g, w_main, w_gate, b_gate, g_norm, w_out)


def _forward(xp, xs, state_pool, mC_s, mn_s, mm_s, W):
    Bp, Sp, _ = xp.shape
    Bs, Ss, _ = xs.shape
    n_pool, n_cell = state_pool.shape[0], mC_s.shape[0]
    hist_p = jnp.zeros((n_pool, Bp, HIST_ROWS, D_MODEL), F32)
    hist_s = jnp.pad(state_pool, ((0, 0), (0, 0), (1, 0), (0, 0)))
    mC_p = jnp.zeros((n_cell, Bp) + mC_s.shape[2:], F32)
    n0_p = jnp.zeros((n_cell, Bp, MLSTM_HEADS, 1, MLSTM_DQK), F32)
    m0_p = jnp.zeros((n_cell, Bp, MLSTM_HEADS, 1, GATE_LANES), F32)
    n0_s = mn_s[:, :, :, None, :]
    m0_s = jnp.broadcast_to(mm_s[:, :, :, None, None], mm_s.shape + (1, GATE_LANES))

    ffn_calls = [(name, i) for i in range(DEPTH) for name in ('ffn1', 'ffn2')]
    w_cur = (W['ffn1_w_in'][0].astype(BF16), W['ffn1_w_out'][0].astype(BF16))

    def run_ffn(k, xp2, xs2, pool=None):
        nonlocal w_cur
        name, layer = ffn_calls[k]
        cv = None
        if k + 1 < len(ffn_calls):
            nxt, nxt_layer = ffn_calls[k + 1]
            cv = (W[nxt + '_w_in'], W[nxt + '_w_out'], nxt_layer)
        out = _ffn(xp2, W['norm_' + name], w_cur[0], w_cur[1], W['norm_final'], layer=layer,
                   final_norm=(k == len(ffn_calls) - 1), tm=FFN_TILE_ROWS, convert=cv, pool=pool, extra_x=xs2,
                   in_place=(k > 0))
        if cv is not None:
            w_cur = (out['w_in'], out['w_out'])
        return out['y'], out['y_extra'], out.get('hist')

    out_p = dict(pool=[], C=[], n=[], m=[])
    out_s = dict(pool=[], C=[], n=[], m=[])
    xp2 = xp.reshape(Bp * Sp, D_MODEL)
    xs2 = xs.reshape(Bs * Ss, D_MODEL)
    for i in range(DEPTH):
        j = i // 2
        if i % 2 == 0:
            xp2, xs2, hist_new = run_ffn(2 * i, xp2, xs2, pool=(hist_p, W['norm_mix'], W['pool_w'], W['pool_b'],
                                                                W['pool_scale'], j, Sp, 0))
            out_p['pool'].append(hist_new[:, 1:])
            xs3, hist_new = _pool_mixer(xs2.reshape(Bs, Ss, D_MODEL), hist_s, W['norm_mix'], W['pool_w'],
                                        W['pool_b'], W['pool_scale'], layer=i, pool_layer=j, pos0=PAST_LEN, ts=Ss)
            xs2 = xs3.reshape(Bs * Ss, D_MODEL)
            out_s['pool'].append(hist_new[:, 1:])
        else:
            xp2, xs2, _ = run_ffn(2 * i, xp2, xs2)
            cell_w = (W['norm_mix'], W['mlstm_w_in'], W['mlstm_w_gate'], W['mlstm_b_gate'], W['mlstm_norm'],
                      W['mlstm_w_out'])
            xp2, C, n, m = _mlstm_mixer(xp2, mC_p, n0_p, m0_p, *cell_w, layer=i, cell_layer=j, seq=Sp,
                                        tl=MLSTM_TILE_ROWS, L=MLSTM_CHUNK, per_chunk_state=False)
            for key, val in (('C', C), ('n', n[:, :, 0, :]), ('m', m[:, :, 0, 0])):
                out_p[key].append(val)
            xs2, C, n, m = _mlstm_mixer(xs2, mC_s, n0_s, m0_s, *cell_w, layer=i, cell_layer=j, seq=Ss,
                                        tl=Bs * Ss, L=Ss, per_chunk_state=True)
            for key, val in (('C', C), ('n', n[:, :, 0, :]), ('m', m[:, :, 0, 0])):
                out_s[key].append(val)
        xp2, xs2, _ = run_ffn(2 * i + 1, xp2, xs2)
    stack = lambda d: tuple(jnp.stack(d[key]) for key in ('pool', 'C', 'n', 'm'))
    return (xp2.reshape(Bp, Sp, D_MODEL), xs2.reshape(Bs, Ss, D_MODEL)) + stack(out_p) + stack(out_s)


def kernel(x_prompt, x_sample, state_pool, state_mlstm_C, state_mlstm_n, state_mlstm_m, norm_ffn1,
           ffn1_w_in, ffn1_w_out, norm_mix, pool_w, pool_b, pool_scale, mlstm_w_in, mlstm_b_i, mlstm_b_f,
           mlstm_norm, mlstm_w_out, norm_ffn2, ffn2_w_in, ffn2_w_out, norm_final):
    n_gate = 2 * MLSTM_HEADS
    W = dict(
        norm_ffn1=norm_ffn1[:, None, :], norm_ffn2=norm_ffn2[:, None, :], norm_mix=norm_mix[:, None, :],
        norm_final=norm_final[None, None, :],
        ffn1_w_in=ffn1_w_in, ffn1_w_out=ffn1_w_out, ffn2_w_in=ffn2_w_in, ffn2_w_out=ffn2_w_out,
        pool_w=pool_w.astype(BF16), pool_b=pool_b[:, None, :], pool_scale=pool_scale[:, None, :],
        mlstm_w_in=mlstm_w_in.astype(BF16),
        mlstm_w_gate=jnp.pad(mlstm_w_in[:, :, MLSTM_MAIN:], ((0, 0), (0, 0), (0, GATE_LANES - n_gate))).astype(BF16),
        mlstm_b_gate=jnp.pad(jnp.concatenate([mlstm_b_i, mlstm_b_f], axis=-1),
                             ((0, 0), (0, GATE_LANES - n_gate)))[:, None, :],
        mlstm_norm=mlstm_norm[:, None, :], mlstm_w_out=mlstm_w_out.astype(BF16),
    )
    return _forward(x_prompt, x_sample, state_pool, state_mlstm_C, state_mlstm_n, state_mlstm_m, W)
```

```python
import functools

import jax
import jax.numpy as jnp
from jax import lax
from jax.experimental import pallas as pl
from jax.experimental.pallas import tpu as pltpu

D_MODEL = 1024
D_FF = 2816
DEPTH = 4
PAST_LEN = 1024
POOL_WINDOWS = (2, 4, 8, 16)
POOL_GROUP_WIDTH = D_MODEL // len(POOL_WINDOWS)
POOL_HIST = max(POOL_WINDOWS) - 1
HIST_ROWS = POOL_HIST + 1
MLSTM_HEADS = 4
MLSTM_DQK = D_MODEL // (2 * MLSTM_HEADS)
MLSTM_DV = D_MODEL // MLSTM_HEADS
MLSTM_QK = MLSTM_HEADS * MLSTM_DQK
MLSTM_V = MLSTM_HEADS * MLSTM_DV
MLSTM_MAIN = 2 * MLSTM_QK + 2 * MLSTM_V
GATE_LANES = 128
EPS = 1e-6

VMEM_LIMIT_BYTES = 56 * 1024 * 1024
FFN_SUB_ROWS = 256
MLSTM_SUB_ROWS = 512

F32 = jnp.float32
BF16 = jnp.bfloat16


def _rmsnorm_f32(x, g):
    return x * lax.rsqrt(jnp.mean(x * x, axis=-1, keepdims=True) + EPS) * g


def _resident(shape, layer, col=0):
    idx = (layer,) + (0,) * (len(shape) - 1) + (col,)
    return pl.BlockSpec((None,) + tuple(shape), lambda *_: idx, pipeline_mode=pl.Buffered(1))


def _pool_rows(x, head, ext_ref, pos_first, g, w_ref, b, sc):
    n = x.shape[0]
    u = _rmsnorm_f32(x, g)
    ext_ref[0:HIST_ROWS, :] = head
    ext_ref[HIST_ROWS:HIST_ROWS + n, :] = u
    ext = ext_ref[...]
    pos = pos_first + lax.broadcasted_iota(jnp.int32, (n, 1), 0)
    ys = []
    for gi, w in enumerate(POOL_WINDOWS):
        c0 = gi * POOL_GROUP_WIDTH
        s = ext[:, c0:c0 + POOL_GROUP_WIDTH]
        k = 1
        while k < w:
            s = s + pltpu.roll(s, k, 0)
            k *= 2
        cnt = jnp.minimum(pos + 1, w).astype(F32)
        d = s[HIST_ROWS:, :] / cnt - u[:, c0:c0 + POOL_GROUP_WIDTH]
        ys.append(jnp.dot(d.astype(BF16), w_ref[gi], preferred_element_type=F32))
    y = (jnp.concatenate(ys, axis=-1) + b) * sc
    return x + y, ext_ref[n:n + HIST_ROWS, :]


def _convert_slab(src_ref, dst_ref, in_buf, out_buf, in_sem, out_sem, *, step, n_steps):
    rows = in_buf.shape[1]
    slot = step % 2

    def in_copy(k, s):
        return pltpu.make_async_copy(src_ref.at[pl.ds(k * rows, rows), :], in_buf.at[s], in_sem.at[s])

    def out_copy(k, s):
        return pltpu.make_async_copy(out_buf.at[s], dst_ref.at[pl.ds(k * rows, rows), :], out_sem.at[s])

    @pl.when(step == 0)
    def _():
        in_copy(0, 0).start()

    in_copy(step, slot).wait()

    @pl.when(step + 1 < n_steps)
    def _():
        in_copy(step + 1, 1 - slot).start()

    @pl.when(step >= 2)
    def _():
        out_copy(step - 2, slot).wait()

    out_buf[slot] = in_buf[slot].astype(BF16)

    def finish():
        out_copy(step, slot).start()

        @pl.when(step == n_steps - 1)
        def _():
            if n_steps >= 2:
                out_copy(step - 1, 1 - slot).wait()
            out_copy(step, slot).wait()

    return finish


def _ffn_body(*refs, final_norm, sub, n_steps, convert_layer, pool):
    it = iter(refs)
    take = lambda n: [next(it) for _ in range(n)]
    do_convert = convert_layer is not None
    x_ref, g_ref, wa_ref, wb_ref, wo_ref, gf_ref = take(6)
    hist_ref, gm_ref, pw_ref, pb_ref, ps_ref = take(5) if pool else [None] * 5
    nwi_ref, nwo_ref = take(2) if do_convert else [None] * 2
    (o_ref,) = take(1)
    (hist_out_ref,) = take(1) if pool else [None]
    cwi_ref, cwo_ref = take(2) if do_convert else [None] * 2
    ext_ref, carry_ref = take(2) if pool else [None] * 2
    step = pl.program_id(0)
    tm = x_ref.shape[0]
    if pool:
        tiles_per_seq, pos0 = pool
        t_in_seq = step % tiles_per_seq

        @pl.when(t_in_seq == 0)
        def _():
            carry_ref[...] = hist_ref[0]

    finishers = []
    if do_convert:
        wi_in, wi_out, wo_in, wo_out, sem_ii, sem_io, sem_oi, sem_oo = take(8)
        finishers = [
            _convert_slab(nwi_ref.at[convert_layer], cwi_ref, wi_in, wi_out, sem_ii, sem_io, step=step, n_steps=n_steps),
            _convert_slab(nwo_ref.at[convert_layer], cwo_ref, wo_in, wo_out, sem_oi, sem_oo, step=step, n_steps=n_steps),
        ]
    def pool_and_store(r0, y):
        y, tail = _pool_rows(y, carry_ref[...], ext_ref, pos0 + t_in_seq * tm + r0, gm_ref[...], pw_ref,
                             pb_ref[...], ps_ref[...])
        carry_ref[...] = tail
        hist_out_ref[0] = tail
        o_ref[r0:r0 + sub, :] = y

    def ffn_rows(x):
        xn = _rmsnorm_f32(x, g_ref[...]).astype(BF16)
        a = jnp.dot(xn, wa_ref[...], preferred_element_type=F32)
        b = jnp.dot(xn, wb_ref[...], preferred_element_type=F32)
        h = (a * jax.nn.sigmoid(a) * b).astype(BF16)
        y = x + 0.5 * jnp.dot(h, wo_ref[...], preferred_element_type=F32)
        return _rmsnorm_f32(y, gf_ref[...]) if final_norm else y

    if pool:
        deferred = None
        for r0 in range(0, tm, sub):
            y = ffn_rows(x_ref[r0:r0 + sub, :])
            if deferred is not None:
                pool_and_store(*deferred)
            deferred = (r0, y)
        pool_and_store(*deferred)
    else:
        def sub_tile(j, carry):
            rows = pl.ds(pl.multiple_of(j * sub, sub), sub)
            o_ref[rows, :] = ffn_rows(x_ref[rows, :])
            return carry

        lax.fori_loop(0, tm // sub, sub_tile, 0)
    for finish in finishers:
        finish()


def _ffn(x, g, w_in, w_out, g_final, *, layer, final_norm, tm, convert=None, pool=None):
    T = x.shape[0]
    assert T % tm == 0
    n_steps = T // tm
    sub = min(tm, FFN_SUB_ROWS)
    row = pl.BlockSpec((tm, D_MODEL), lambda i: (i, 0))
    in_specs = [
        row,
        _resident((1, D_MODEL), layer),
        pl.BlockSpec((D_MODEL, D_FF), lambda i: (0, 0), pipeline_mode=pl.Buffered(1)),
        pl.BlockSpec((D_MODEL, D_FF), lambda i: (0, 1), pipeline_mode=pl.Buffered(1)),
        pl.BlockSpec((D_FF, D_MODEL), lambda i: (0, 0), pipeline_mode=pl.Buffered(1)),
        _resident((1, D_MODEL), 0),
    ]
    out_specs = [row]
    out_shape = [jax.ShapeDtypeStruct((T, D_MODEL), F32)]
    operands = [x, g, w_in, w_in, w_out, g_final]
    scratch = []
    pool_cfg = None
    if pool is not None:
        hist, g_mix, pw, pb, psc, pool_layer, seq, pos0 = pool
        assert seq % tm == 0 and sub >= HIST_ROWS
        tiles_per_seq = seq // tm
        pool_cfg = (tiles_per_seq, pos0)
        in_specs += [
            pl.BlockSpec((None, 1, HIST_ROWS, D_MODEL), lambda i: (pool_layer, i // tiles_per_seq, 0, 0)),
            _resident((1, D_MODEL), layer),
            _resident(pw.shape[1:], pool_layer),
            _resident((1, D_MODEL), pool_layer),
            _resident((1, D_MODEL), pool_layer),
        ]
        out_specs += [pl.BlockSpec((1, HIST_ROWS, D_MODEL), lambda i: (i // tiles_per_seq, 0, 0))]
        out_shape += [jax.ShapeDtypeStruct((T // seq, HIST_ROWS, D_MODEL), F32)]
        operands += [hist, g_mix, pw, pb, psc]
        scratch += [pltpu.VMEM((HIST_ROWS + sub, D_MODEL), F32), pltpu.VMEM((HIST_ROWS, D_MODEL), F32)]
    convert_layer = None
    if convert is not None:
        src_in, src_out, convert_layer = convert
        assert D_MODEL % n_steps == 0 and D_FF % n_steps == 0
        r_in, r_out = D_MODEL // n_steps, D_FF // n_steps
        any_spec = pl.BlockSpec(memory_space=pl.ANY)
        in_specs += [any_spec, any_spec]
        out_specs += [any_spec, any_spec]
        out_shape += [jax.ShapeDtypeStruct((D_MODEL, 2 * D_FF), BF16), jax.ShapeDtypeStruct((D_FF, D_MODEL), BF16)]
        operands += [src_in, src_out]
        scratch += [
            pltpu.VMEM((2, r_in, 2 * D_FF), F32), pltpu.VMEM((2, r_in, 2 * D_FF), BF16),
            pltpu.VMEM((2, r_out, D_MODEL), F32), pltpu.VMEM((2, r_out, D_MODEL), BF16),
        ] + [pltpu.SemaphoreType.DMA((2,))] * 4
    return pl.pallas_call(
        functools.partial(_ffn_body, final_norm=final_norm, sub=sub, n_steps=n_steps,
                          convert_layer=convert_layer, pool=pool_cfg),
        grid=(n_steps,),
        in_specs=in_specs,
        out_specs=out_specs,
        out_shape=out_shape,
        scratch_shapes=scratch,
        compiler_params=pltpu.CompilerParams(
            dimension_semantics=("arbitrary",), vmem_limit_bytes=VMEM_LIMIT_BYTES),
        name="ffn",
    )(*operands)


def _pool_body(x_ref, hist_ref, g_ref, w_ref, b_ref, sc_ref, o_ref, hist_out_ref, ext_ref, carry_ref,
               *, ts, pos0):
    t = pl.program_id(1)

    @pl.when(t == 0)
    def _():
        carry_ref[...] = hist_ref[0]

    y, tail = _pool_rows(x_ref[0], carry_ref[...], ext_ref, pos0 + t * ts, g_ref[...], w_ref, b_ref[...],
                         sc_ref[...])
    o_ref[0] = y
    carry_ref[...] = tail
    hist_out_ref[0] = tail


def _pool_mixer(x, hist, g, w, b, sc, *, layer, pool_layer, pos0, ts):
    B, S, _ = x.shape
    assert S % ts == 0 and ts >= HIST_ROWS
    return pl.pallas_call(
        functools.partial(_pool_body, ts=ts, pos0=pos0),
        grid=(B, S // ts),
        in_specs=[
            pl.BlockSpec((1, ts, D_MODEL), lambda bi, t: (bi, t, 0)),
            pl.BlockSpec((None, 1, HIST_ROWS, D_MODEL), lambda bi, t: (pool_layer, bi, 0, 0)),
            _resident((1, D_MODEL), layer),
            _resident(w.shape[1:], pool_layer),
            _resident((1, D_MODEL), pool_layer),
            _resident((1, D_MODEL), pool_layer),
        ],
        out_specs=[
            pl.BlockSpec((1, ts, D_MODEL), lambda bi, t: (bi, t, 0)),
            pl.BlockSpec((1, HIST_ROWS, D_MODEL), lambda bi, t: (bi, 0, 0)),
        ],
        out_shape=[
            jax.ShapeDtypeStruct((B, S, D_MODEL), F32),
            jax.ShapeDtypeStruct((B, HIST_ROWS, D_MODEL), F32),
        ],
        scratch_shapes=[pltpu.VMEM((HIST_ROWS + ts, D_MODEL), F32), pltpu.VMEM((HIST_ROWS, D_MODEL), F32)],
        compiler_params=pltpu.CompilerParams(
            dimension_semantics=("arbitrary", "arbitrary"), vmem_limit_bytes=VMEM_LIMIT_BYTES),
        name="pool_mixer",
    )(x, hist, g, w, b, sc)


def _log_sigmoid(x):
    return jnp.minimum(x, 0.0) - jnp.log1p(jnp.exp(-jnp.abs(x)))


def _dot_nt(a, b):
    return lax.dot_general(a, b, (((1,), (1,)), ((), ())), preferred_element_type=F32)


def _dot_tn(a, b):
    return lax.dot_general(a, b, (((0,), (0,)), ((), ())), preferred_element_type=F32)


def _mlstm_body(x_ref, c0_ref, n0_ref, m0_ref, g_ref, win_ref, wg_ref, bg_ref, gn_ref, wo_ref,
                o_ref, c_ref, n_ref, m_ref, *scratch, tl, L, per_chunk_state, sub):
    H = MLSTM_HEADS
    t = pl.program_id(1)
    n_sub = tl // sub
    p_ss, gate_ss, hn_ss = scratch[:n_sub], scratch[n_sub:2 * n_sub], scratch[2 * n_sub:]

    @pl.when(t == 0)
    def _():
        c_ref[...] = c0_ref[...]
        n_ref[...] = n0_ref[...]
        m_ref[...] = m0_ref[...]

    for i_sub in range(n_sub):
        x = x_ref[i_sub * sub:(i_sub + 1) * sub, :]
        u = _rmsnorm_f32(x, g_ref[...]).astype(BF16)
        p_ss[i_sub][...] = jnp.dot(u, win_ref[...], preferred_element_type=F32)
        pre = jnp.dot(u, wg_ref[...], preferred_element_type=F32) + bg_ref[...]
        lane = lax.broadcasted_iota(jnp.int32, pre.shape, 1)
        gate_ss[i_sub][...] = jnp.where(lane < H, pre, _log_sigmoid(pre))

    row_g = lax.broadcasted_iota(jnp.int32, (L, GATE_LANES), 0)
    causal = lax.broadcasted_iota(jnp.int32, (L, L), 0) >= lax.broadcasted_iota(jnp.int32, (L, L), 1)
    for c in range(tl // L):
        i_sub, r0 = divmod(c * L, sub)
        p_s, gate_s, hn_s = p_ss[i_sub], gate_ss[i_sub], hn_ss[i_sub]
        si = c if per_chunk_state else 0
        G = gate_s[r0:r0 + L, :]
        Bc = G
        k = 1
        while k < L:
            Bc = Bc + jnp.where(row_g >= k, pltpu.roll(Bc, k, 0), 0.0)
            k *= 2
        A = G - pltpu.roll(Bc, GATE_LANES - H, 1)
        AT = A.T
        for h in range(H):
            bcol = Bc[:, H + h:H + h + 1]
            acol = A[:, h:h + 1]
            arow = AT[h:h + 1, :]
            C = c_ref[si, h]
            n = n_ref[si, h]
            m_prev = m_ref[si, h][:, 0:1]
            q = p_s[r0:r0 + L, h * MLSTM_DQK:(h + 1) * MLSTM_DQK] * (MLSTM_DQK ** -0.5)
            kk = p_s[r0:r0 + L, MLSTM_QK + h * MLSTM_DQK:MLSTM_QK + (h + 1) * MLSTM_DQK]
            v0 = 2 * MLSTM_QK + h * MLSTM_DV
            v = p_s[r0:r0 + L, v0:v0 + MLSTM_DV]
            og = p_s[r0:r0 + L, v0 + MLSTM_V:v0 + MLSTM_V + MLSTM_DV]
            qb = q.astype(BF16)
            kb = kk.astype(BF16)

            log_d = jnp.where(causal, bcol + arow, -jnp.inf)
            log_state = bcol + m_prev
            m_t = jnp.maximum(jnp.max(log_d, axis=-1, keepdims=True), log_state)
            w_intra = jnp.exp(log_d - m_t)
            w_state = jnp.exp(log_state - m_t)
            s = _dot_nt(qb, kb) * w_intra
            num = (jnp.dot(s.astype(BF16), v.astype(BF16), preferred_element_type=F32)
                   + w_state * _dot_nt(qb, C.astype(BF16)))
            den = jnp.sum(s, axis=-1, keepdims=True) + w_state * jnp.sum(q * n, axis=-1, keepdims=True)
            hh = num / jnp.maximum(jnp.abs(den), jnp.exp(-m_t))
            hh = hh * lax.rsqrt(jnp.mean(hh * hh, axis=-1, keepdims=True) + EPS)
            hh = hh * gn_ref[:, h * MLSTM_DV:(h + 1) * MLSTM_DV]
            hn_s[r0:r0 + L, h * MLSTM_DV:(h + 1) * MLSTM_DV] = (jax.nn.sigmoid(og) * hh).astype(BF16)

            m_new = m_t[L - 1:L, :]
            b_last = bcol[L - 1:L, :]
            w_carry = jnp.exp(b_last + m_prev - m_new)
            w_write = jnp.exp(b_last + acol - m_new)
            c_ref[si, h] = w_carry * C + _dot_tn((v * w_write).astype(BF16), kb)
            n_ref[si, h] = w_carry * n + jnp.sum(kk * w_write, axis=0, keepdims=True)
            m_ref[si, h] = jnp.broadcast_to(m_new, (1, GATE_LANES))

        if r0 + L == sub:
            rows = slice(i_sub * sub, (i_sub + 1) * sub)
            o_ref[rows, :] = x_ref[rows, :] + jnp.dot(hn_s[...], wo_ref[...], preferred_element_type=F32)


def _mlstm_mixer(x, C0, n0, m0, g, w_main, w_gate, b_gate, g_norm, w_out,
                 *, layer, cell_layer, seq, tl, L, per_chunk_state):
    T = x.shape[0]
    H = MLSTM_HEADS
    B = C0.shape[1]
    jl = cell_layer
    sub = min(tl, MLSTM_SUB_ROWS)
    n_sub = tl // sub
    nb = B if per_chunk_state else 1
    tiles_per_seq = 1 if per_chunk_state else seq // tl
    if per_chunk_state:
        assert seq == L and tl == nb * L and T == tl
    else:
        assert seq % tl == 0 and T == B * seq
    assert tl % sub == 0 and sub % L == 0
    row = pl.BlockSpec((tl, D_MODEL), lambda b, t: (b * tiles_per_seq + t, 0))
    c_spec = pl.BlockSpec((nb, H, MLSTM_DV, MLSTM_DQK), lambda b, t: (b, 0, 0, 0))
    n_spec = pl.BlockSpec((nb, H, 1, MLSTM_DQK), lambda b, t: (b, 0, 0, 0))
    m_spec = pl.BlockSpec((nb, H, 1, GATE_LANES), lambda b, t: (b, 0, 0, 0))
    return pl.pallas_call(
        functools.partial(_mlstm_body, tl=tl, L=L, per_chunk_state=per_chunk_state, sub=sub),
        grid=(B // nb, tiles_per_seq),
        in_specs=[
            row,
            pl.BlockSpec((None, nb, H, MLSTM_DV, MLSTM_DQK), lambda b, t: (jl, b, 0, 0, 0)),
            pl.BlockSpec((None, nb, H, 1, MLSTM_DQK), lambda b, t: (jl, b, 0, 0, 0)),
            pl.BlockSpec((None, nb, H, 1, GATE_LANES), lambda b, t: (jl, b, 0, 0, 0)),
            _resident((1, D_MODEL), layer),
            _resident((D_MODEL, MLSTM_MAIN), jl),
            _resident((D_MODEL, GATE_LANES), jl),
            _resident((1, GATE_LANES), jl),
            _resident((1, MLSTM_V), jl),
            _resident((MLSTM_V, D_MODEL), jl),
        ],
        out_specs=[row, c_spec, n_spec, m_spec],
        out_shape=[
            jax.ShapeDtypeStruct((T, D_MODEL), F32),
            jax.ShapeDtypeStruct((B, H, MLSTM_DV, MLSTM_DQK), F32),
            jax.ShapeDtypeStruct((B, H, 1, MLSTM_DQK), F32),
            jax.ShapeDtypeStruct((B, H, 1, GATE_LANES), F32),
        ],
        scratch_shapes=([pltpu.VMEM((sub, MLSTM_MAIN), F32)] * n_sub
                        + [pltpu.VMEM((sub, GATE_LANES), F32)] * n_sub
                        + [pltpu.VMEM((sub, MLSTM_V), BF16)] * n_sub),
        compiler_params=pltpu.CompilerParams(
            dimension_semantics=("arbitrary", "arbitrary"), vmem_limit_bytes=VMEM_LIMIT_BYTES),
        name="mlstm_layer",
    )(x, C0, n0, m0, g, w_main, w_gate, b_gate, g_norm, w_out)


def _trunk(x, pool_hist, mC, mn, mm, pos0, W, ffn_w, *, tm, ts, tl, L, per_chunk_state, convert, fuse_pool):
    B, S, _ = x.shape
    pool_new, C_new, n_new, m_new = [], [], [], []
    hist = jnp.pad(pool_hist, ((0, 0), (0, 0), (1, 0), (0, 0)))
    n0 = mn[:, :, :, None, :]
    m0 = jnp.broadcast_to(mm[:, :, :, None, None], mm.shape + (1, GATE_LANES))
    ffn_calls = [(name, i) for i in range(DEPTH) for name in ('ffn1', 'ffn2')]

    def run_ffn(k, x2, pool=None):
        name, layer = ffn_calls[k]
        cv = None
        if convert and k + 1 < len(ffn_calls):
            nxt, nxt_layer = ffn_calls[k + 1]
            cv = (W[nxt + '_w_in'], W[nxt + '_w_out'], nxt_layer)
        out = _ffn(x2, W['norm_' + name], ffn_w[k][0], ffn_w[k][1], W['norm_final'], layer=layer,
                   final_norm=(k == len(ffn_calls) - 1), tm=tm, convert=cv, pool=pool)
        if cv is not None:
            ffn_w.append((out[-2], out[-1]))
        return out[0], (out[1] if pool is not None else None)

    x2 = x.reshape(B * S, D_MODEL)
    for i in range(DEPTH):
        j = i // 2
        if i % 2 == 0 and fuse_pool:
            x2, hist_new = run_ffn(2 * i, x2, pool=(hist, W['norm_mix'], W['pool_w'], W['pool_b'], W['pool_scale'],
                                                    j, S, pos0))
            pool_new.append(hist_new[:, 1:])
        elif i % 2 == 0:
            x2, _ = run_ffn(2 * i, x2)
            x3, hist_new = _pool_mixer(x2.reshape(B, S, D_MODEL), hist, W['norm_mix'], W['pool_w'],
                                       W['pool_b'], W['pool_scale'], layer=i, pool_layer=j, pos0=pos0, ts=ts)
            x2 = x3.reshape(B * S, D_MODEL)
            pool_new.append(hist_new[:, 1:])
        else:
            x2, _ = run_ffn(2 * i, x2)
            x2, C, n, m = _mlstm_mixer(x2, mC, n0, m0, W['norm_mix'], W['mlstm_w_in'], W['mlstm_w_gate'],
                                       W['mlstm_b_gate'], W['mlstm_norm'], W['mlstm_w_out'],
                                       layer=i, cell_layer=j, seq=S, tl=tl, L=L, per_chunk_state=per_chunk_state)
            C_new.append(C)
            n_new.append(n[:, :, 0, :])
            m_new.append(m[:, :, 0, 0])
        x2, _ = run_ffn(2 * i + 1, x2)
    return (x2.reshape(B, S, D_MODEL), jnp.stack(pool_new), jnp.stack(C_new), jnp.stack(n_new),
            jnp.stack(m_new))


def kernel(x_prompt, x_sample, state_pool, state_mlstm_C, state_mlstm_n, state_mlstm_m, norm_ffn1,
           ffn1_w_in, ffn1_w_out, norm_mix, pool_w, pool_b, pool_scale, mlstm_w_in, mlstm_b_i, mlstm_b_f,
           mlstm_norm, mlstm_w_out, norm_ffn2, ffn2_w_in, ffn2_w_out, norm_final):
    n_gate = 2 * MLSTM_HEADS
    W = dict(
        norm_ffn1=norm_ffn1[:, None, :], norm_ffn2=norm_ffn2[:, None, :], norm_mix=norm_mix[:, None, :],
        norm_final=norm_final[None, None, :],
        ffn1_w_in=ffn1_w_in, ffn1_w_out=ffn1_w_out, ffn2_w_in=ffn2_w_in, ffn2_w_out=ffn2_w_out,
        pool_w=pool_w.astype(BF16), pool_b=pool_b[:, None, :], pool_scale=pool_scale[:, None, :],
        mlstm_w_in=mlstm_w_in.astype(BF16),
        mlstm_w_gate=jnp.pad(mlstm_w_in[:, :, MLSTM_MAIN:], ((0, 0), (0, 0), (0, GATE_LANES - n_gate))).astype(BF16),
        mlstm_b_gate=jnp.pad(jnp.concatenate([mlstm_b_i, mlstm_b_f], axis=-1),
                             ((0, 0), (0, GATE_LANES - n_gate)))[:, None, :],
        mlstm_norm=mlstm_norm[:, None, :], mlstm_w_out=mlstm_w_out.astype(BF16),
    )
    B = x_prompt.shape[0]
    zero_pool = jnp.zeros((state_pool.shape[0], B) + state_pool.shape[2:], state_pool.dtype)
    zero_C = jnp.zeros((state_mlstm_C.shape[0], B) + state_mlstm_C.shape[2:], state_mlstm_C.dtype)
    zero_n = jnp.zeros((state_mlstm_n.shape[0], B) + state_mlstm_n.shape[2:], state_mlstm_n.dtype)
    zero_m = jnp.zeros((state_mlstm_m.shape[0], B) + state_mlstm_m.shape[2:], state_mlstm_m.dtype)
    ffn_w = [(ffn1_w_in[0].astype(BF16), ffn1_w_out[0].astype(BF16))]
    y_p, pool_p, C_p, n_p, m_p = _trunk(x_prompt, zero_pool, zero_C, zero_n, zero_m, 0, W, ffn_w,
                                        tm=1024, ts=512, tl=1024, L=256, per_chunk_state=False, convert=True,
                                        fuse_pool=True)
    Bs, Ss, _ = x_sample.shape
    y_s, pool_s, C_s, n_s, m_s = _trunk(x_sample, state_pool, state_mlstm_C, state_mlstm_n, state_mlstm_m,
                                        PAST_LEN, W, ffn_w, tm=Bs * Ss, ts=Ss, tl=Bs * Ss, L=Ss,
                                        per_chunk_state=True, convert=False, fuse_pool=False)
    return (y_p, y_s, pool_p, C_p, n_p, m_p, pool_s, C_s, n_s, m_s)
```

```python
import functools

import jax
import jax.numpy as jnp
from jax import lax
from jax.experimental import pallas as pl
from jax.experimental.pallas import tpu as pltpu

D_MODEL = 1024
D_FF = 2816
DEPTH = 4
PAST_LEN = 1024
POOL_WINDOWS = (2, 4, 8, 16)
POOL_GROUP_WIDTH = D_MODEL // len(POOL_WINDOWS)
POOL_HIST = max(POOL_WINDOWS) - 1
HIST_ROWS = POOL_HIST + 1
MLSTM_HEADS = 4
MLSTM_DQK = D_MODEL // (2 * MLSTM_HEADS)
MLSTM_DV = D_MODEL // MLSTM_HEADS
MLSTM_QK = MLSTM_HEADS * MLSTM_DQK
MLSTM_V = MLSTM_HEADS * MLSTM_DV
MLSTM_MAIN = 2 * MLSTM_QK + 2 * MLSTM_V
GATE_LANES = 128
EPS = 1e-6

VMEM_LIMIT_BYTES = 56 * 1024 * 1024
FFN_SUB_ROWS = 256
MLSTM_SUB_ROWS = 256

F32 = jnp.float32
BF16 = jnp.bfloat16


def _rmsnorm_f32(x, g):
    return x * lax.rsqrt(jnp.mean(x * x, axis=-1, keepdims=True) + EPS) * g


def _resident(shape, layer, col=0):
    idx = (layer,) + (0,) * (len(shape) - 1) + (col,)
    return pl.BlockSpec((None,) + tuple(shape), lambda *_: idx, pipeline_mode=pl.Buffered(1))


def _pool_rows(x, head, ext_ref, pos_first, g, w_ref, b, sc):
    n = x.shape[0]
    u = _rmsnorm_f32(x, g)
    ext_ref[0:HIST_ROWS, :] = head
    ext_ref[HIST_ROWS:HIST_ROWS + n, :] = u
    ext = ext_ref[...]
    pos = pos_first + lax.broadcasted_iota(jnp.int32, (n, 1), 0)
    ys = []
    for gi, w in enumerate(POOL_WINDOWS):
        c0 = gi * POOL_GROUP_WIDTH
        s = ext[:, c0:c0 + POOL_GROUP_WIDTH]
        k = 1
        while k < w:
            s = s + pltpu.roll(s, k, 0)
            k *= 2
        cnt = jnp.minimum(pos + 1, w).astype(F32)
        d = s[HIST_ROWS:, :] / cnt - u[:, c0:c0 + POOL_GROUP_WIDTH]
        ys.append(jnp.dot(d.astype(BF16), w_ref[gi], preferred_element_type=F32))
    y = (jnp.concatenate(ys, axis=-1) + b) * sc
    return x + y, ext_ref[n:n + HIST_ROWS, :]


def _convert_slab(src_ref, dst_ref, in_buf, out_buf, in_sem, out_sem, *, step, n_steps):
    rows = in_buf.shape[1]
    slot = step % 2

    def in_copy(k, s):
        return pltpu.make_async_copy(src_ref.at[pl.ds(k * rows, rows), :], in_buf.at[s], in_sem.at[s])

    def out_copy(k, s):
        return pltpu.make_async_copy(out_buf.at[s], dst_ref.at[pl.ds(k * rows, rows), :], out_sem.at[s])

    @pl.when(step == 0)
    def _():
        in_copy(0, 0).start()

    in_copy(step, slot).wait()

    @pl.when(step + 1 < n_steps)
    def _():
        in_copy(step + 1, 1 - slot).start()

    @pl.when(step >= 2)
    def _():
        out_copy(step - 2, slot).wait()

    out_buf[slot] = in_buf[slot].astype(BF16)

    def finish():
        out_copy(step, slot).start()

        @pl.when(step == n_steps - 1)
        def _():
            if n_steps >= 2:
                out_copy(step - 1, 1 - slot).wait()
            out_copy(step, slot).wait()

    return finish


def _ffn_body(*refs, final_norm, sub, n_steps, convert_layer, pool):
    it = iter(refs)
    take = lambda n: [next(it) for _ in range(n)]
    do_convert = convert_layer is not None
    x_ref, g_ref, wa_ref, wb_ref, wo_ref, gf_ref = take(6)
    hist_ref, gm_ref, pw_ref, pb_ref, ps_ref = take(5) if pool else [None] * 5
    nwi_ref, nwo_ref = take(2) if do_convert else [None] * 2
    (o_ref,) = take(1)
    (hist_out_ref,) = take(1) if pool else [None]
    cwi_ref, cwo_ref = take(2) if do_convert else [None] * 2
    ext_ref, carry_ref = take(2) if pool else [None] * 2
    step = pl.program_id(0)
    tm = x_ref.shape[0]
    if pool:
        tiles_per_seq, pos0 = pool
        t_in_seq = step % tiles_per_seq

        @pl.when(t_in_seq == 0)
        def _():
            carry_ref[...] = hist_ref[0]

    finishers = []
    if do_convert:
        wi_in, wi_out, wo_in, wo_out, sem_ii, sem_io, sem_oi, sem_oo = take(8)
        finishers = [
            _convert_slab(nwi_ref.at[convert_layer], cwi_ref, wi_in, wi_out, sem_ii, sem_io, step=step, n_steps=n_steps),
            _convert_slab(nwo_ref.at[convert_layer], cwo_ref, wo_in, wo_out, sem_oi, sem_oo, step=step, n_steps=n_steps),
        ]
    def pool_and_store(r0, y):
        y, tail = _pool_rows(y, carry_ref[...], ext_ref, pos0 + t_in_seq * tm + r0, gm_ref[...], pw_ref,
                             pb_ref[...], ps_ref[...])
        carry_ref[...] = tail
        hist_out_ref[0] = tail
        o_ref[r0:r0 + sub, :] = y

    deferred = None
    for r0 in range(0, tm, sub):
        x = x_ref[r0:r0 + sub, :]
        xn = _rmsnorm_f32(x, g_ref[...]).astype(BF16)
        a = jnp.dot(xn, wa_ref[...], preferred_element_type=F32)
        b = jnp.dot(xn, wb_ref[...], preferred_element_type=F32)
        h = (a * jax.nn.sigmoid(a) * b).astype(BF16)
        y = x + 0.5 * jnp.dot(h, wo_ref[...], preferred_element_type=F32)
        if final_norm:
            y = _rmsnorm_f32(y, gf_ref[...])
        if pool:
            if deferred is not None:
                pool_and_store(*deferred)
            deferred = (r0, y)
        else:
            o_ref[r0:r0 + sub, :] = y
    if deferred is not None:
        pool_and_store(*deferred)
    for finish in finishers:
        finish()


def _ffn(x, g, w_in, w_out, g_final, *, layer, final_norm, tm, convert=None, pool=None):
    T = x.shape[0]
    assert T % tm == 0
    n_steps = T // tm
    sub = min(tm, FFN_SUB_ROWS)
    row = pl.BlockSpec((tm, D_MODEL), lambda i: (i, 0))
    in_specs = [
        row,
        _resident((1, D_MODEL), layer),
        pl.BlockSpec((D_MODEL, D_FF), lambda i: (0, 0), pipeline_mode=pl.Buffered(1)),
        pl.BlockSpec((D_MODEL, D_FF), lambda i: (0, 1), pipeline_mode=pl.Buffered(1)),
        pl.BlockSpec((D_FF, D_MODEL), lambda i: (0, 0), pipeline_mode=pl.Buffered(1)),
        _resident((1, D_MODEL), 0),
    ]
    out_specs = [row]
    out_shape = [jax.ShapeDtypeStruct((T, D_MODEL), F32)]
    operands = [x, g, w_in, w_in, w_out, g_final]
    scratch = []
    pool_cfg = None
    if pool is not None:
        hist, g_mix, pw, pb, psc, pool_layer, seq, pos0 = pool
        assert seq % tm == 0 and sub >= HIST_ROWS
        tiles_per_seq = seq // tm
        pool_cfg = (tiles_per_seq, pos0)
        in_specs += [
            pl.BlockSpec((None, 1, HIST_ROWS, D_MODEL), lambda i: (pool_layer, i // tiles_per_seq, 0, 0)),
            _resident((1, D_MODEL), layer),
            _resident(pw.shape[1:], pool_layer),
            _resident((1, D_MODEL), pool_layer),
            _resident((1, D_MODEL), pool_layer),
        ]
        out_specs += [pl.BlockSpec((1, HIST_ROWS, D_MODEL), lambda i: (i // tiles_per_seq, 0, 0))]
        out_shape += [jax.ShapeDtypeStruct((T // seq, HIST_ROWS, D_MODEL), F32)]
        operands += [hist, g_mix, pw, pb, psc]
        scratch += [pltpu.VMEM((HIST_ROWS + sub, D_MODEL), F32), pltpu.VMEM((HIST_ROWS, D_MODEL), F32)]
    convert_layer = None
    if convert is not None:
        src_in, src_out, convert_layer = convert
        assert D_MODEL % n_steps == 0 and D_FF % n_steps == 0
        r_in, r_out = D_MODEL // n_steps, D_FF // n_steps
        any_spec = pl.BlockSpec(memory_space=pl.ANY)
        in_specs += [any_spec, any_spec]
        out_specs += [any_spec, any_spec]
        out_shape += [jax.ShapeDtypeStruct((D_MODEL, 2 * D_FF), BF16), jax.ShapeDtypeStruct((D_FF, D_MODEL), BF16)]
        operands += [src_in, src_out]
        scratch += [
            pltpu.VMEM((2, r_in, 2 * D_FF), F32), pltpu.VMEM((2, r_in, 2 * D_FF), BF16),
            pltpu.VMEM((2, r_out, D_MODEL), F32), pltpu.VMEM((2, r_out, D_MODEL), BF16),
        ] + [pltpu.SemaphoreType.DMA((2,))] * 4
    return pl.pallas_call(
        functools.partial(_ffn_body, final_norm=final_norm, sub=sub, n_steps=n_steps,
                          convert_layer=convert_layer, pool=pool_cfg),
        grid=(n_steps,),
        in_specs=in_specs,
        out_specs=out_specs,
        out_shape=out_shape,
        scratch_shapes=scratch,
        compiler_params=pltpu.CompilerParams(
            dimension_semantics=("arbitrary",), vmem_limit_bytes=VMEM_LIMIT_BYTES),
        name="ffn",
    )(*operands)


def _pool_body(x_ref, hist_ref, g_ref, w_ref, b_ref, sc_ref, o_ref, hist_out_ref, ext_ref, carry_ref,
               *, ts, pos0):
    t = pl.program_id(1)

    @pl.when(t == 0)
    def _():
        carry_ref[...] = hist_ref[0]

    y, tail = _pool_rows(x_ref[0], carry_ref[...], ext_ref, pos0 + t * ts, g_ref[...], w_ref, b_ref[...],
                         sc_ref[...])
    o_ref[0] = y
    carry_ref[...] = tail
    hist_out_ref[0] = tail


def _pool_mixer(x, hist, g, w, b, sc, *, layer, pool_layer, pos0, ts):
    B, S, _ = x.shape
    assert S % ts == 0 and ts >= HIST_ROWS
    return pl.pallas_call(
        functools.partial(_pool_body, ts=ts, pos0=pos0),
        grid=(B, S // ts),
        in_specs=[
            pl.BlockSpec((1, ts, D_MODEL), lambda bi, t: (bi, t, 0)),
            pl.BlockSpec((None, 1, HIST_ROWS, D_MODEL), lambda bi, t: (pool_layer, bi, 0, 0)),
            _resident((1, D_MODEL), layer),
            _resident(w.shape[1:], pool_layer),
            _resident((1, D_MODEL), pool_layer),
            _resident((1, D_MODEL), pool_layer),
        ],
        out_specs=[
            pl.BlockSpec((1, ts, D_MODEL), lambda bi, t: (bi, t, 0)),
            pl.BlockSpec((1, HIST_ROWS, D_MODEL), lambda bi, t: (bi, 0, 0)),
        ],
        out_shape=[
            jax.ShapeDtypeStruct((B, S, D_MODEL), F32),
            jax.ShapeDtypeStruct((B, HIST_ROWS, D_MODEL), F32),
        ],
        scratch_shapes=[pltpu.VMEM((HIST_ROWS + ts, D_MODEL), F32), pltpu.VMEM((HIST_ROWS, D_MODEL), F32)],
        compiler_params=pltpu.CompilerParams(
            dimension_semantics=("arbitrary", "arbitrary"), vmem_limit_bytes=VMEM_LIMIT_BYTES),
        name="pool_mixer",
    )(x, hist, g, w, b, sc)


def _log_sigmoid(x):
    return jnp.minimum(x, 0.0) - jnp.log1p(jnp.exp(-jnp.abs(x)))


def _dot_nt(a, b):
    return lax.dot_general(a, b, (((1,), (1,)), ((), ())), preferred_element_type=F32)


def _dot_tn(a, b):
    return lax.dot_general(a, b, (((0,), (0,)), ((), ())), preferred_element_type=F32)


def _mlstm_body(x_ref, c0_ref, n0_ref, m0_ref, g_ref, win_ref, wg_ref, bg_ref, gn_ref, wo_ref,
                o_ref, c_ref, n_ref, m_ref, *scratch, tl, L, per_chunk_state, sub):
    H = MLSTM_HEADS
    t = pl.program_id(1)
    n_sub = tl // sub
    p_ss, gate_ss, hn_ss = scratch[:n_sub], scratch[n_sub:2 * n_sub], scratch[2 * n_sub:]

    @pl.when(t == 0)
    def _():
        c_ref[...] = c0_ref[...]
        n_ref[...] = n0_ref[...]
        m_ref[...] = m0_ref[...]

    for i_sub in range(n_sub):
        x = x_ref[i_sub * sub:(i_sub + 1) * sub, :]
        u = _rmsnorm_f32(x, g_ref[...]).astype(BF16)
        p_ss[i_sub][...] = jnp.dot(u, win_ref[...], preferred_element_type=F32)
        pre = jnp.dot(u, wg_ref[...], preferred_element_type=F32) + bg_ref[...]
        lane = lax.broadcasted_iota(jnp.int32, pre.shape, 1)
        gate_ss[i_sub][...] = jnp.where(lane < H, pre, _log_sigmoid(pre))

    row_g = lax.broadcasted_iota(jnp.int32, (L, GATE_LANES), 0)
    causal = lax.broadcasted_iota(jnp.int32, (L, L), 0) >= lax.broadcasted_iota(jnp.int32, (L, L), 1)
    for c in range(tl // L):
        i_sub, r0 = divmod(c * L, sub)
        p_s, gate_s, hn_s = p_ss[i_sub], gate_ss[i_sub], hn_ss[i_sub]
        si = c if per_chunk_state else 0
        G = gate_s[r0:r0 + L, :]
        Bc = G
        k = 1
        while k < L:
            Bc = Bc + jnp.where(row_g >= k, pltpu.roll(Bc, k, 0), 0.0)
            k *= 2
        A = G - pltpu.roll(Bc, GATE_LANES - H, 1)
        AT = A.T
        for h in range(H):
            bcol = Bc[:, H + h:H + h + 1]
            acol = A[:, h:h + 1]
            arow = AT[h:h + 1, :]
            C = c_ref[si, h]
            n = n_ref[si, h]
            m_prev = m_ref[si, h][:, 0:1]
            q = p_s[r0:r0 + L, h * MLSTM_DQK:(h + 1) * MLSTM_DQK] * (MLSTM_DQK ** -0.5)
            kk = p_s[r0:r0 + L, MLSTM_QK + h * MLSTM_DQK:MLSTM_QK + (h + 1) * MLSTM_DQK]
            v0 = 2 * MLSTM_QK + h * MLSTM_DV
            v = p_s[r0:r0 + L, v0:v0 + MLSTM_DV]
            og = p_s[r0:r0 + L, v0 + MLSTM_V:v0 + MLSTM_V + MLSTM_DV]
            qb = q.astype(BF16)
            kb = kk.astype(BF16)

            log_d = jnp.where(causal, bcol + arow, -jnp.inf)
            log_state = bcol + m_prev
            m_t = jnp.maximum(jnp.max(log_d, axis=-1, keepdims=True), log_state)
            w_intra = jnp.exp(log_d - m_t)
            w_state = jnp.exp(log_state - m_t)
            s = _dot_nt(qb, kb) * w_intra
            num = (jnp.dot(s.astype(BF16), v.astype(BF16), preferred_element_type=F32)
                   + w_state * _dot_nt(qb, C.astype(BF16)))
            den = jnp.sum(s, axis=-1, keepdims=True) + w_state * jnp.sum(q * n, axis=-1, keepdims=True)
            hh = num / jnp.maximum(jnp.abs(den), jnp.exp(-m_t))
            hh = hh * lax.rsqrt(jnp.mean(hh * hh, axis=-1, keepdims=True) + EPS)
            hh = hh * gn_ref[:, h * MLSTM_DV:(h + 1) * MLSTM_DV]
            hn_s[r0:r0 + L, h * MLSTM_DV:(h + 1) * MLSTM_DV] = (jax.nn.sigmoid(og) * hh).astype(BF16)

            m_new = m_t[L - 1:L, :]
            b_last = bcol[L - 1:L, :]
            w_carry = jnp.exp(b_last + m_prev - m_new)
            w_write = jnp.exp(b_last + acol - m_new)
            c_ref[si, h] = w_carry * C + _dot_tn((v * w_write).astype(BF16), kb)
            n_ref[si, h] = w_carry * n + jnp.sum(kk * w_write, axis=0, keepdims=True)
            m_ref[si, h] = jnp.broadcast_to(m_new, (1, GATE_LANES))

        if r0 + L == sub:
            rows = slice(i_sub * sub, (i_sub + 1) * sub)
            o_ref[rows, :] = x_ref[rows, :] + jnp.dot(hn_s[...], wo_ref[...], preferred_element_type=F32)


def _mlstm_mixer(x, C0, n0, m0, g, w_main, w_gate, b_gate, g_norm, w_out,
                 *, layer, cell_layer, seq, tl, L, per_chunk_state):
    T = x.shape[0]
    H = MLSTM_HEADS
    B = C0.shape[1]
    jl = cell_layer
    sub = min(tl, MLSTM_SUB_ROWS)
    n_sub = tl // sub
    nb = B if per_chunk_state else 1
    tiles_per_seq = 1 if per_chunk_state else seq // tl
    if per_chunk_state:
        assert seq == L and tl == nb * L and T == tl
    else:
        assert seq % tl == 0 and T == B * seq
    assert tl % sub == 0 and sub % L == 0
    row = pl.BlockSpec((tl, D_MODEL), lambda b, t: (b * tiles_per_seq + t, 0))
    c_spec = pl.BlockSpec((nb, H, MLSTM_DV, MLSTM_DQK), lambda b, t: (b, 0, 0, 0))
    n_spec = pl.BlockSpec((nb, H, 1, MLSTM_DQK), lambda b, t: (b, 0, 0, 0))
    m_spec = pl.BlockSpec((nb, H, 1, GATE_LANES), lambda b, t: (b, 0, 0, 0))
    return pl.pallas_call(
        functools.partial(_mlstm_body, tl=tl, L=L, per_chunk_state=per_chunk_state, sub=sub),
        grid=(B // nb, tiles_per_seq),
        in_specs=[
            row,
            pl.BlockSpec((None, nb, H, MLSTM_DV, MLSTM_DQK), lambda b, t: (jl, b, 0, 0, 0)),
            pl.BlockSpec((None, nb, H, 1, MLSTM_DQK), lambda b, t: (jl, b, 0, 0, 0)),
            pl.BlockSpec((None, nb, H, 1, GATE_LANES), lambda b, t: (jl, b, 0, 0, 0)),
            _resident((1, D_MODEL), layer),
            _resident((D_MODEL, MLSTM_MAIN), jl),
            _resident((D_MODEL, GATE_LANES), jl),
            _resident((1, GATE_LANES), jl),
            _resident((1, MLSTM_V), jl),
            _resident((MLSTM_V, D_MODEL), jl),
        ],
        out_specs=[row, c_spec, n_spec, m_spec],
        out_shape=[
            jax.ShapeDtypeStruct((T, D_MODEL), F32),
            jax.ShapeDtypeStruct((B, H, MLSTM_DV, MLSTM_DQK), F32),
            jax.ShapeDtypeStruct((B, H, 1, MLSTM_DQK), F32),
            jax.ShapeDtypeStruct((B, H, 1, GATE_LANES), F32),
        ],
        scratch_shapes=([pltpu.VMEM((sub, MLSTM_MAIN), F32)] * n_sub
                        + [pltpu.VMEM((sub, GATE_LANES), F32)] * n_sub
                        + [pltpu.VMEM((sub, MLSTM_V), BF16)] * n_sub),
        compiler_params=pltpu.CompilerParams(
            dimension_semantics=("arbitrary", "arbitrary"), vmem_limit_bytes=VMEM_LIMIT_BYTES),
        name="mlstm_layer",
    )(x, C0, n0, m0, g, w_main, w_gate, b_gate, g_norm, w_out)


def _trunk(x, pool_hist, mC, mn, mm, pos0, W, ffn_w, *, tm, ts, tl, L, per_chunk_state, convert, fuse_pool):
    B, S, _ = x.shape
    pool_new, C_new, n_new, m_new = [], [], [], []
    hist = jnp.pad(pool_hist, ((0, 0), (0, 0), (1, 0), (0, 0)))
    n0 = mn[:, :, :, None, :]
    m0 = jnp.broadcast_to(mm[:, :, :, None, None], mm.shape + (1, GATE_LANES))
    ffn_calls = [(name, i) for i in range(DEPTH) for name in ('ffn1', 'ffn2')]

    def run_ffn(k, x2, pool=None):
        name, layer = ffn_calls[k]
        cv = None
        if convert and k + 1 < len(ffn_calls):
            nxt, nxt_layer = ffn_calls[k + 1]
            cv = (W[nxt + '_w_in'], W[nxt + '_w_out'], nxt_layer)
        out = _ffn(x2, W['norm_' + name], ffn_w[k][0], ffn_w[k][1], W['norm_final'], layer=layer,
                   final_norm=(k == len(ffn_calls) - 1), tm=tm, convert=cv, pool=pool)
        if cv is not None:
            ffn_w.append((out[-2], out[-1]))
        return out[0], (out[1] if pool is not None else None)

    x2 = x.reshape(B * S, D_MODEL)
    for i in range(DEPTH):
        j = i // 2
        if i % 2 == 0 and fuse_pool:
            x2, hist_new = run_ffn(2 * i, x2, pool=(hist, W['norm_mix'], W['pool_w'], W['pool_b'], W['pool_scale'],
                                                    j, S, pos0))
            pool_new.append(hist_new[:, 1:])
        elif i % 2 == 0:
            x2, _ = run_ffn(2 * i, x2)
            x3, hist_new = _pool_mixer(x2.reshape(B, S, D_MODEL), hist, W['norm_mix'], W['pool_w'],
                                       W['pool_b'], W['pool_scale'], layer=i, pool_layer=j, pos0=pos0, ts=ts)
            x2 = x3.reshape(B * S, D_MODEL)
            pool_new.append(hist_new[:, 1:])
        else:
            x2, _ = run_ffn(2 * i, x2)
            x2, C, n, m = _mlstm_mixer(x2, mC, n0, m0, W['norm_mix'], W['mlstm_w_in'], W['mlstm_w_gate'],
                                       W['mlstm_b_gate'], W['mlstm_norm'], W['mlstm_w_out'],
                                       layer=i, cell_layer=j, seq=S, tl=tl, L=L, per_chunk_state=per_chunk_state)
            C_new.append(C)
            n_new.append(n[:, :, 0, :])
            m_new.append(m[:, :, 0, 0])
        x2, _ = run_ffn(2 * i + 1, x2)
    return (x2.reshape(B, S, D_MODEL), jnp.stack(pool_new), jnp.stack(C_new), jnp.stack(n_new),
            jnp.stack(m_new))


def kernel(x_prompt, x_sample, state_pool, state_mlstm_C, state_mlstm_n, state_mlstm_m, norm_ffn1,
           ffn1_w_in, ffn1_w_out, norm_mix, pool_w, pool_b, pool_scale, mlstm_w_in, mlstm_b_i, mlstm_b_f,
           mlstm_norm, mlstm_w_out, norm_ffn2, ffn2_w_in, ffn2_w_out, norm_final):
    n_gate = 2 * MLSTM_HEADS
    W = dict(
        norm_ffn1=norm_ffn1[:, None, :], norm_ffn2=norm_ffn2[:, None, :], norm_mix=norm_mix[:, None, :],
        norm_final=norm_final[None, None, :],
        ffn1_w_in=ffn1_w_in, ffn1_w_out=ffn1_w_out, ffn2_w_in=ffn2_w_in, ffn2_w_out=ffn2_w_out,
        pool_w=pool_w.astype(BF16), pool_b=pool_b[:, None, :], pool_scale=pool_scale[:, None, :],
        mlstm_w_in=mlstm_w_in[:, :, :MLSTM_MAIN].astype(BF16),
        mlstm_w_gate=jnp.pad(mlstm_w_in[:, :, MLSTM_MAIN:], ((0, 0), (0, 0), (0, GATE_LANES - n_gate))).astype(BF16),
        mlstm_b_gate=jnp.pad(jnp.concatenate([mlstm_b_i, mlstm_b_f], axis=-1),
                             ((0, 0), (0, GATE_LANES - n_gate)))[:, None, :],
        mlstm_norm=mlstm_norm[:, None, :], mlstm_w_out=mlstm_w_out.astype(BF16),
    )
    B = x_prompt.shape[0]
    zero_pool = jnp.zeros((state_pool.shape[0], B) + state_pool.shape[2:], state_pool.dtype)
    zero_C = jnp.zeros((state_mlstm_C.shape[0], B) + state_mlstm_C.shape[2:], state_mlstm_C.dtype)
    zero_n = jnp.zeros((state_mlstm_n.shape[0], B) + state_mlstm_n.shape[2:], state_mlstm_n.dtype)
    zero_m = jnp.zeros((state_mlstm_m.shape[0], B) + state_mlstm_m.shape[2:], state_mlstm_m.dtype)
    ffn_w = [(ffn1_w_in[0].astype(BF16), ffn1_w_out[0].astype(BF16))]
    y_p, pool_p, C_p, n_p, m_p = _trunk(x_prompt, zero_pool, zero_C, zero_n, zero_m, 0, W, ffn_w,
                                        tm=1024, ts=512, tl=1024, L=256, per_chunk_state=False, convert=True,
                                        fuse_pool=True)
    Bs, Ss, _ = x_sample.shape
    y_s, pool_s, C_s, n_s, m_s = _trunk(x_sample, state_pool, state_mlstm_C, state_mlstm_n, state_mlstm_m,
                                        PAST_LEN, W, ffn_w, tm=Bs * Ss, ts=Ss, tl=Bs * Ss, L=Ss,
                                        per_chunk_state=True, convert=False, fuse_pool=False)
    return (y_p, y_s, pool_p, C_p, n_p, m_p, pool_s, C_s, n_s, m_s)
```

```python
import functools

import jax
import jax.numpy as jnp
from jax import lax
from jax.experimental import pallas as pl
from jax.experimental.pallas import tpu as pltpu

D_MODEL = 1024
D_FF = 2816
DEPTH = 4
PAST_LEN = 1024
POOL_WINDOWS = (2, 4, 8, 16)
POOL_GROUP_WIDTH = D_MODEL // len(POOL_WINDOWS)
POOL_HIST = max(POOL_WINDOWS) - 1
HIST_ROWS = POOL_HIST + 1
MLSTM_HEADS = 4
MLSTM_DQK = D_MODEL // (2 * MLSTM_HEADS)
MLSTM_DV = D_MODEL // MLSTM_HEADS
MLSTM_QK = MLSTM_HEADS * MLSTM_DQK
MLSTM_V = MLSTM_HEADS * MLSTM_DV
MLSTM_MAIN = 2 * MLSTM_QK + 2 * MLSTM_V
GATE_LANES = 128
EPS = 1e-6

VMEM_LIMIT_BYTES = 56 * 1024 * 1024
FFN_SUB_ROWS = 256
MLSTM_SUB_ROWS = 512
FFN_TILE_ROWS = 1024
MLSTM_TILE_ROWS = 1024
MLSTM_CHUNK = 256

F32 = jnp.float32
BF16 = jnp.bfloat16


def _rmsnorm_f32(x, g):
    return x * lax.rsqrt(jnp.mean(x * x, axis=-1, keepdims=True) + EPS) * g


def _resident(shape, layer, col=0):
    idx = (layer,) + (0,) * (len(shape) - 1) + (col,)
    return pl.BlockSpec((None,) + tuple(shape), lambda *_: idx, pipeline_mode=pl.Buffered(1))


def _pool_rows(x, head, ext_ref, pos_first, g, w_ref, b, sc):
    n = x.shape[0]
    u = _rmsnorm_f32(x, g)
    ext_ref[0:HIST_ROWS, :] = head
    ext_ref[HIST_ROWS:HIST_ROWS + n, :] = u
    ext = ext_ref[...]
    pos = pos_first + lax.broadcasted_iota(jnp.int32, (n, 1), 0)
    ys = []
    for gi, w in enumerate(POOL_WINDOWS):
        c0 = gi * POOL_GROUP_WIDTH
        s = ext[:, c0:c0 + POOL_GROUP_WIDTH]
        k = 1
        while k < w:
            s = s + pltpu.roll(s, k, 0)
            k *= 2
        cnt = jnp.minimum(pos + 1, w).astype(F32)
        d = s[HIST_ROWS:, :] / cnt - u[:, c0:c0 + POOL_GROUP_WIDTH]
        ys.append(jnp.dot(d.astype(BF16), w_ref[gi], preferred_element_type=F32))
    y = (jnp.concatenate(ys, axis=-1) + b) * sc
    return x + y, ext_ref[n:n + HIST_ROWS, :]


def _convert_slab(src_ref, dst_ref, in_buf, out_buf, in_sem, out_sem, *, step, n_steps):
    rows = in_buf.shape[1]
    slot = step % 2

    def in_copy(k, s):
        return pltpu.make_async_copy(src_ref.at[pl.ds(k * rows, rows), :], in_buf.at[s], in_sem.at[s])

    def out_copy(k, s):
        return pltpu.make_async_copy(out_buf.at[s], dst_ref.at[pl.ds(k * rows, rows), :], out_sem.at[s])

    @pl.when(step == 0)
    def _():
        in_copy(0, 0).start()

    in_copy(step, slot).wait()

    @pl.when(step + 1 < n_steps)
    def _():
        in_copy(step + 1, 1 - slot).start()

    @pl.when(step >= 2)
    def _():
        out_copy(step - 2, slot).wait()

    out_buf[slot] = in_buf[slot].astype(BF16)

    def finish():
        out_copy(step, slot).start()

        @pl.when(step == n_steps - 1)
        def _():
            if n_steps >= 2:
                out_copy(step - 1, 1 - slot).wait()
            out_copy(step, slot).wait()

    return finish


def _ffn_body(*refs, final_norm, sub, n_steps, convert_layer, pool, extra):
    it = iter(refs)
    take = lambda n: [next(it) for _ in range(n)]
    do_convert = convert_layer is not None
    x_ref, g_ref, wa_ref, wb_ref, wo_ref, gf_ref = take(6)
    (xs_ref,) = take(1) if extra else [None]
    hist_ref, gm_ref, pw_ref, pb_ref, ps_ref = take(5) if pool else [None] * 5
    nwi_ref, nwo_ref = take(2) if do_convert else [None] * 2
    (o_ref,) = take(1)
    (os_ref,) = take(1) if extra else [None]
    (hist_out_ref,) = take(1) if pool else [None]
    cwi_ref, cwo_ref = take(2) if do_convert else [None] * 2
    ext_ref, carry_ref = take(2) if pool else [None] * 2
    conv_scratch = take(8) if do_convert else None
    step = pl.program_id(0)
    tm = x_ref.shape[0]

    def ffn_rows(x):
        xn = _rmsnorm_f32(x, g_ref[...]).astype(BF16)
        a = jnp.dot(xn, wa_ref[...], preferred_element_type=F32)
        b = jnp.dot(xn, wb_ref[...], preferred_element_type=F32)
        h = (a * jax.nn.sigmoid(a) * b).astype(BF16)
        y = x + 0.5 * jnp.dot(h, wo_ref[...], preferred_element_type=F32)
        return _rmsnorm_f32(y, gf_ref[...]) if final_norm else y

    def main_step():
        if pool:
            tiles_per_seq, pos0 = pool
            t_in_seq = step % tiles_per_seq

            @pl.when(t_in_seq == 0)
            def _():
                carry_ref[...] = hist_ref[0]

        finishers = []
        if do_convert:
            wi_in, wi_out, wo_in, wo_out, sem_ii, sem_io, sem_oi, sem_oo = conv_scratch
            finishers = [
                _convert_slab(nwi_ref.at[convert_layer], cwi_ref, wi_in, wi_out, sem_ii, sem_io, step=step,
                              n_steps=n_steps),
                _convert_slab(nwo_ref.at[convert_layer], cwo_ref, wo_in, wo_out, sem_oi, sem_oo, step=step,
                              n_steps=n_steps),
            ]

        def pool_and_store(r0, y):
            y, tail = _pool_rows(y, carry_ref[...], ext_ref, pos0 + t_in_seq * tm + r0, gm_ref[...], pw_ref,
                                 pb_ref[...], ps_ref[...])
            carry_ref[...] = tail
            hist_out_ref[0] = tail
            o_ref[r0:r0 + sub, :] = y

        deferred = None
        for r0 in range(0, tm, sub):
            y = ffn_rows(x_ref[r0:r0 + sub, :])
            if pool:
                if deferred is not None:
                    pool_and_store(*deferred)
                deferred = (r0, y)
            else:
                o_ref[r0:r0 + sub, :] = y
        if deferred is not None:
            pool_and_store(*deferred)
        for finish in finishers:
            finish()

    if not extra:
        main_step()
        return
    pl.when(step < n_steps)(main_step)

    @pl.when(step == n_steps)
    def _():
        def sub_tile(j, carry):
            rows = pl.ds(pl.multiple_of(j * sub, sub), sub)
            os_ref[rows, :] = ffn_rows(xs_ref[rows, :])
            return carry

        lax.fori_loop(0, xs_ref.shape[0] // sub, sub_tile, 0)


def _ffn(x, g, w_in, w_out, g_final, *, layer, final_norm, tm, convert=None, pool=None, extra_x=None):
    T = x.shape[0]
    assert T % tm == 0
    n_steps = T // tm
    sub = min(tm, FFN_SUB_ROWS)
    last = n_steps - 1
    row = pl.BlockSpec((tm, D_MODEL), lambda i: (jnp.minimum(i, last), 0))
    in_specs = [
        row,
        _resident((1, D_MODEL), layer),
        pl.BlockSpec((D_MODEL, D_FF), lambda i: (0, 0), pipeline_mode=pl.Buffered(1)),
        pl.BlockSpec((D_MODEL, D_FF), lambda i: (0, 1), pipeline_mode=pl.Buffered(1)),
        pl.BlockSpec((D_FF, D_MODEL), lambda i: (0, 0), pipeline_mode=pl.Buffered(1)),
        _resident((1, D_MODEL), 0),
    ]
    out_specs = [row]
    out_shape = [jax.ShapeDtypeStruct((T, D_MODEL), F32)]
    operands = [x, g, w_in, w_in, w_out, g_final]
    names = ['y']
    scratch = []
    if extra_x is not None:
        Ts = extra_x.shape[0]
        assert Ts % sub == 0
        in_specs += [pl.BlockSpec((Ts, D_MODEL), lambda i: (0, 0), pipeline_mode=pl.Buffered(1))]
        out_specs += [pl.BlockSpec((Ts, D_MODEL), lambda i: (0, 0))]
        out_shape += [jax.ShapeDtypeStruct((Ts, D_MODEL), F32)]
        operands += [extra_x]
        names += ['y_extra']
    pool_cfg = None
    if pool is not None:
        hist, g_mix, pw, pb, psc, pool_layer, seq, pos0 = pool
        assert seq % tm == 0 and sub >= HIST_ROWS
        tiles_per_seq = seq // tm
        pool_cfg = (tiles_per_seq, pos0)
        in_specs += [
            pl.BlockSpec((None, 1, HIST_ROWS, D_MODEL),
                         lambda i: (pool_layer, jnp.minimum(i, last) // tiles_per_seq, 0, 0)),
            _resident((1, D_MODEL), layer),
            _resident(pw.shape[1:], pool_layer),
            _resident((1, D_MODEL), pool_layer),
            _resident((1, D_MODEL), pool_layer),
        ]
        out_specs += [pl.BlockSpec((1, HIST_ROWS, D_MODEL), lambda i: (jnp.minimum(i, last) // tiles_per_seq, 0, 0))]
        out_shape += [jax.ShapeDtypeStruct((T // seq, HIST_ROWS, D_MODEL), F32)]
        operands += [hist, g_mix, pw, pb, psc]
        names += ['hist']
        scratch += [pltpu.VMEM((HIST_ROWS + sub, D_MODEL), F32), pltpu.VMEM((HIST_ROWS, D_MODEL), F32)]
    convert_layer = None
    if convert is not None:
        src_in, src_out, convert_layer = convert
        assert D_MODEL % n_steps == 0 and D_FF % n_steps == 0
        r_in, r_out = D_MODEL // n_steps, D_FF // n_steps
        any_spec = pl.BlockSpec(memory_space=pl.ANY)
        in_specs += [any_spec, any_spec]
        out_specs += [any_spec, any_spec]
        out_shape += [jax.ShapeDtypeStruct((D_MODEL, 2 * D_FF), BF16), jax.ShapeDtypeStruct((D_FF, D_MODEL), BF16)]
        operands += [src_in, src_out]
        names += ['w_in', 'w_out']
        scratch += [
            pltpu.VMEM((2, r_in, 2 * D_FF), F32), pltpu.VMEM((2, r_in, 2 * D_FF), BF16),
            pltpu.VMEM((2, r_out, D_MODEL), F32), pltpu.VMEM((2, r_out, D_MODEL), BF16),
        ] + [pltpu.SemaphoreType.DMA((2,))] * 4
    outs = pl.pallas_call(
        functools.partial(_ffn_body, final_norm=final_norm, sub=sub, n_steps=n_steps,
                          convert_layer=convert_layer, pool=pool_cfg, extra=extra_x is not None),
        grid=(n_steps + (extra_x is not None),),
        in_specs=in_specs,
        out_specs=out_specs,
        out_shape=out_shape,
        scratch_shapes=scratch,
        compiler_params=pltpu.CompilerParams(
            dimension_semantics=("arbitrary",), vmem_limit_bytes=VMEM_LIMIT_BYTES),
        name="ffn",
    )(*operands)
    return dict(zip(names, outs))


def _pool_body(x_ref, hist_ref, g_ref, w_ref, b_ref, sc_ref, o_ref, hist_out_ref, ext_ref, carry_ref,
               *, ts, pos0):
    t = pl.program_id(1)

    @pl.when(t == 0)
    def _():
        carry_ref[...] = hist_ref[0]

    y, tail = _pool_rows(x_ref[0], carry_ref[...], ext_ref, pos0 + t * ts, g_ref[...], w_ref, b_ref[...],
                         sc_ref[...])
    o_ref[0] = y
    carry_ref[...] = tail
    hist_out_ref[0] = tail


def _pool_mixer(x, hist, g, w, b, sc, *, layer, pool_layer, pos0, ts):
    B, S, _ = x.shape
    assert S % ts == 0 and ts >= HIST_ROWS
    return pl.pallas_call(
        functools.partial(_pool_body, ts=ts, pos0=pos0),
        grid=(B, S // ts),
        in_specs=[
            pl.BlockSpec((1, ts, D_MODEL), lambda bi, t: (bi, t, 0)),
            pl.BlockSpec((None, 1, HIST_ROWS, D_MODEL), lambda bi, t: (pool_layer, bi, 0, 0)),
            _resident((1, D_MODEL), layer),
            _resident(w.shape[1:], pool_layer),
            _resident((1, D_MODEL), pool_layer),
            _resident((1, D_MODEL), pool_layer),
        ],
        out_specs=[
            pl.BlockSpec((1, ts, D_MODEL), lambda bi, t: (bi, t, 0)),
            pl.BlockSpec((1, HIST_ROWS, D_MODEL), lambda bi, t: (bi, 0, 0)),
        ],
        out_shape=[
            jax.ShapeDtypeStruct((B, S, D_MODEL), F32),
            jax.ShapeDtypeStruct((B, HIST_ROWS, D_MODEL), F32),
        ],
        scratch_shapes=[pltpu.VMEM((HIST_ROWS + ts, D_MODEL), F32), pltpu.VMEM((HIST_ROWS, D_MODEL), F32)],
        compiler_params=pltpu.CompilerParams(
            dimension_semantics=("arbitrary", "arbitrary"), vmem_limit_bytes=VMEM_LIMIT_BYTES),
        name="pool_mixer",
    )(x, hist, g, w, b, sc)


def _log_sigmoid(x):
    return jnp.minimum(x, 0.0) - jnp.log1p(jnp.exp(-jnp.abs(x)))


def _dot_nt(a, b):
    return lax.dot_general(a, b, (((1,), (1,)), ((), ())), preferred_element_type=F32)


def _dot_tn(a, b):
    return lax.dot_general(a, b, (((0,), (0,)), ((), ())), preferred_element_type=F32)


def _mlstm_body(x_ref, c0_ref, n0_ref, m0_ref, g_ref, win_ref, wg_ref, bg_ref, gn_ref, wo_ref,
                o_ref, c_ref, n_ref, m_ref, *scratch, tl, L, per_chunk_state, sub):
    H = MLSTM_HEADS
    t = pl.program_id(1)
    n_sub = tl // sub
    p_ss, gate_ss, hn_ss = scratch[:n_sub], scratch[n_sub:2 * n_sub], scratch[2 * n_sub:]

    @pl.when(t == 0)
    def _():
        c_ref[...] = c0_ref[...]
        n_ref[...] = n0_ref[...]
        m_ref[...] = m0_ref[...]

    for i_sub in range(n_sub):
        x = x_ref[i_sub * sub:(i_sub + 1) * sub, :]
        u = _rmsnorm_f32(x, g_ref[...]).astype(BF16)
        p_ss[i_sub][...] = jnp.dot(u, win_ref[...], preferred_element_type=F32)
        pre = jnp.dot(u, wg_ref[...], preferred_element_type=F32) + bg_ref[...]
        lane = lax.broadcasted_iota(jnp.int32, pre.shape, 1)
        gate_ss[i_sub][...] = jnp.where(lane < H, pre, _log_sigmoid(pre))

    row_g = lax.broadcasted_iota(jnp.int32, (L, GATE_LANES), 0)
    causal = lax.broadcasted_iota(jnp.int32, (L, L), 0) >= lax.broadcasted_iota(jnp.int32, (L, L), 1)
    for c in range(tl // L):
        i_sub, r0 = divmod(c * L, sub)
        p_s, gate_s, hn_s = p_ss[i_sub], gate_ss[i_sub], hn_ss[i_sub]
        si = c if per_chunk_state else 0
        G = gate_s[r0:r0 + L, :]
        Bc = G
        k = 1
        while k < L:
            Bc = Bc + jnp.where(row_g >= k, pltpu.roll(Bc, k, 0), 0.0)
            k *= 2
        A = G - pltpu.roll(Bc, GATE_LANES - H, 1)
        AT = A.T
        for h in range(H):
            bcol = Bc[:, H + h:H + h + 1]
            acol = A[:, h:h + 1]
            arow = AT[h:h + 1, :]
            C = c_ref[si, h]
            n = n_ref[si, h]
            m_prev = m_ref[si, h][:, 0:1]
            q = p_s[r0:r0 + L, h * MLSTM_DQK:(h + 1) * MLSTM_DQK] * (MLSTM_DQK ** -0.5)
            kk = p_s[r0:r0 + L, MLSTM_QK + h * MLSTM_DQK:MLSTM_QK + (h + 1) * MLSTM_DQK]
            v0 = 2 * MLSTM_QK + h * MLSTM_DV
            v = p_s[r0:r0 + L, v0:v0 + MLSTM_DV]
            og = p_s[r0:r0 + L, v0 + MLSTM_V:v0 + MLSTM_V + MLSTM_DV]
            qb = q.astype(BF16)
            kb = kk.astype(BF16)

            log_d = jnp.where(causal, bcol + arow, -jnp.inf)
            log_state = bcol + m_prev
            m_t = jnp.maximum(jnp.max(log_d, axis=-1, keepdims=True), log_state)
            w_intra = jnp.exp(log_d - m_t)
            w_state = jnp.exp(log_state - m_t)
            s = _dot_nt(qb, kb) * w_intra
            num = (jnp.dot(s.astype(BF16), v.astype(BF16), preferred_element_type=F32)
                   + w_state * _dot_nt(qb, C.astype(BF16)))
            den = jnp.sum(s, axis=-1, keepdims=True) + w_state * jnp.sum(q * n, axis=-1, keepdims=True)
            hh = num / jnp.maximum(jnp.abs(den), jnp.exp(-m_t))
            hh = hh * lax.rsqrt(jnp.mean(hh * hh, axis=-1, keepdims=True) + EPS)
            hh = hh * gn_ref[:, h * MLSTM_DV:(h + 1) * MLSTM_DV]
            hn_s[r0:r0 + L, h * MLSTM_DV:(h + 1) * MLSTM_DV] = (jax.nn.sigmoid(og) * hh).astype(BF16)

            m_new = m_t[L - 1:L, :]
            b_last = bcol[L - 1:L, :]
            w_carry = jnp.exp(b_last + m_prev - m_new)
            w_write = jnp.exp(b_last + acol - m_new)
            c_ref[si, h] = w_carry * C + _dot_tn((v * w_write).astype(BF16), kb)
            n_ref[si, h] = w_carry * n + jnp.sum(kk * w_write, axis=0, keepdims=True)
            m_ref[si, h] = jnp.broadcast_to(m_new, (1, GATE_LANES))

        if r0 + L == sub:
            rows = slice(i_sub * sub, (i_sub + 1) * sub)
            o_ref[rows, :] = x_ref[rows, :] + jnp.dot(hn_s[...], wo_ref[...], preferred_element_type=F32)


def _mlstm_mixer(x, C0, n0, m0, g, w_main, w_gate, b_gate, g_norm, w_out,
                 *, layer, cell_layer, seq, tl, L, per_chunk_state):
    T = x.shape[0]
    H = MLSTM_HEADS
    B = C0.shape[1]
    jl = cell_layer
    sub = min(tl, MLSTM_SUB_ROWS)
    n_sub = tl // sub
    nb = B if per_chunk_state else 1
    tiles_per_seq = 1 if per_chunk_state else seq // tl
    if per_chunk_state:
        assert seq == L and tl == nb * L and T == tl
    else:
        assert seq % tl == 0 and T == B * seq
    assert tl % sub == 0 and sub % L == 0
    row = pl.BlockSpec((tl, D_MODEL), lambda b, t: (b * tiles_per_seq + t, 0))
    c_spec = pl.BlockSpec((nb, H, MLSTM_DV, MLSTM_DQK), lambda b, t: (b, 0, 0, 0))
    n_spec = pl.BlockSpec((nb, H, 1, MLSTM_DQK), lambda b, t: (b, 0, 0, 0))
    m_spec = pl.BlockSpec((nb, H, 1, GATE_LANES), lambda b, t: (b, 0, 0, 0))
    return pl.pallas_call(
        functools.partial(_mlstm_body, tl=tl, L=L, per_chunk_state=per_chunk_state, sub=sub),
        grid=(B // nb, tiles_per_seq),
        in_specs=[
            row,
            pl.BlockSpec((None, nb, H, MLSTM_DV, MLSTM_DQK), lambda b, t: (jl, b, 0, 0, 0)),
            pl.BlockSpec((None, nb, H, 1, MLSTM_DQK), lambda b, t: (jl, b, 0, 0, 0)),
            pl.BlockSpec((None, nb, H, 1, GATE_LANES), lambda b, t: (jl, b, 0, 0, 0)),
            _resident((1, D_MODEL), layer),
            _resident((D_MODEL, MLSTM_MAIN), jl),
            _resident((D_MODEL, GATE_LANES), jl),
            _resident((1, GATE_LANES), jl),
            _resident((1, MLSTM_V), jl),
            _resident((MLSTM_V, D_MODEL), jl),
        ],
        out_specs=[row, c_spec, n_spec, m_spec],
        out_shape=[
            jax.ShapeDtypeStruct((T, D_MODEL), F32),
            jax.ShapeDtypeStruct((B, H, MLSTM_DV, MLSTM_DQK), F32),
            jax.ShapeDtypeStruct((B, H, 1, MLSTM_DQK), F32),
            jax.ShapeDtypeStruct((B, H, 1, GATE_LANES), F32),
        ],
        scratch_shapes=([pltpu.VMEM((sub, MLSTM_MAIN), F32)] * n_sub
                        + [pltpu.VMEM((sub, GATE_LANES), F32)] * n_sub
                        + [pltpu.VMEM((sub, MLSTM_V), BF16)] * n_sub),
        compiler_params=pltpu.CompilerParams(
            dimension_semantics=("arbitrary", "arbitrary"), vmem_limit_bytes=VMEM_LIMIT_BYTES),
        name="mlstm_layer",
    )(x, C0, n0, m0, g, w_main, w_gate, b_gate, g_norm, w_out)


def _forward(xp, xs, state_pool, mC_s, mn_s, mm_s, W):
    Bp, Sp, _ = xp.shape
    Bs, Ss, _ = xs.shape
    n_pool, n_cell = state_pool.shape[0], mC_s.shape[0]
    hist_p = jnp.zeros((n_pool, Bp, HIST_ROWS, D_MODEL), F32)
    hist_s = jnp.pad(state_pool, ((0, 0), (0, 0), (1, 0), (0, 0)))
    mC_p = jnp.zeros((n_cell, Bp) + mC_s.shape[2:], F32)
    n0_p = jnp.zeros((n_cell, Bp, MLSTM_HEADS, 1, MLSTM_DQK), F32)
    m0_p = jnp.zeros((n_cell, Bp, MLSTM_HEADS, 1, GATE_LANES), F32)
    n0_s = mn_s[:, :, :, None, :]
    m0_s = jnp.broadcast_to(mm_s[:, :, :, None, None], mm_s.shape + (1, GATE_LANES))

    ffn_calls = [(name, i) for i in range(DEPTH) for name in ('ffn1', 'ffn2')]
    w_cur = (W['ffn1_w_in'][0].astype(BF16), W['ffn1_w_out'][0].astype(BF16))

    def run_ffn(k, xp2, xs2, pool=None):
        nonlocal w_cur
        name, layer = ffn_calls[k]
        cv = None
        if k + 1 < len(ffn_calls):
            nxt, nxt_layer = ffn_calls[k + 1]
            cv = (W[nxt + '_w_in'], W[nxt + '_w_out'], nxt_layer)
        out = _ffn(xp2, W['norm_' + name], w_cur[0], w_cur[1], W['norm_final'], layer=layer,
                   final_norm=(k == len(ffn_calls) - 1), tm=FFN_TILE_ROWS, convert=cv, pool=pool, extra_x=xs2)
        if cv is not None:
            w_cur = (out['w_in'], out['w_out'])
        return out['y'], out['y_extra'], out.get('hist')

    out_p = dict(pool=[], C=[], n=[], m=[])
    out_s = dict(pool=[], C=[], n=[], m=[])
    xp2 = xp.reshape(Bp * Sp, D_MODEL)
    xs2 = xs.reshape(Bs * Ss, D_MODEL)
    for i in range(DEPTH):
        j = i // 2
        if i % 2 == 0:
            xp2, xs2, hist_new = run_ffn(2 * i, xp2, xs2, pool=(hist_p, W['norm_mix'], W['pool_w'], W['pool_b'],
                                                                W['pool_scale'], j, Sp, 0))
            out_p['pool'].append(hist_new[:, 1:])
            xs3, hist_new = _pool_mixer(xs2.reshape(Bs, Ss, D_MODEL), hist_s, W['norm_mix'], W['pool_w'],
                                        W['pool_b'], W['pool_scale'], layer=i, pool_layer=j, pos0=PAST_LEN, ts=Ss)
            xs2 = xs3.reshape(Bs * Ss, D_MODEL)
            out_s['pool'].append(hist_new[:, 1:])
        else:
            xp2, xs2, _ = run_ffn(2 * i, xp2, xs2)
            cell_w = (W['norm_mix'], W['mlstm_w_in'], W['mlstm_w_gate'], W['mlstm_b_gate'], W['mlstm_norm'],
                      W['mlstm_w_out'])
            xp2, C, n, m = _mlstm_mixer(xp2, mC_p, n0_p, m0_p, *cell_w, layer=i, cell_layer=j, seq=Sp,
                                        tl=MLSTM_TILE_ROWS, L=MLSTM_CHUNK, per_chunk_state=False)
            for key, val in (('C', C), ('n', n[:, :, 0, :]), ('m', m[:, :, 0, 0])):
                out_p[key].append(val)
            xs2, C, n, m = _mlstm_mixer(xs2, mC_s, n0_s, m0_s, *cell_w, layer=i, cell_layer=j, seq=Ss,
                                        tl=Bs * Ss, L=Ss, per_chunk_state=True)
            for key, val in (('C', C), ('n', n[:, :, 0, :]), ('m', m[:, :, 0, 0])):
                out_s[key].append(val)
        xp2, xs2, _ = run_ffn(2 * i + 1, xp2, xs2)
    stack = lambda d: tuple(jnp.stack(d[key]) for key in ('pool', 'C', 'n', 'm'))
    return (xp2.reshape(Bp, Sp, D_MODEL), xs2.reshape(Bs, Ss, D_MODEL)) + stack(out_p) + stack(out_s)


def kernel(x_prompt, x_sample, state_pool, state_mlstm_C, state_mlstm_n, state_mlstm_m, norm_ffn1,
           ffn1_w_in, ffn1_w_out, norm_mix, pool_w, pool_b, pool_scale, mlstm_w_in, mlstm_b_i, mlstm_b_f,
           mlstm_norm, mlstm_w_out, norm_ffn2, ffn2_w_in, ffn2_w_out, norm_final):
    n_gate = 2 * MLSTM_HEADS
    W = dict(
        norm_ffn1=norm_ffn1[:, None, :], norm_ffn2=norm_ffn2[:, None, :], norm_mix=norm_mix[:, None, :],
        norm_final=norm_final[None, None, :],
        ffn1_w_in=ffn1_w_in, ffn1_w_out=ffn1_w_out, ffn2_w_in=ffn2_w_in, ffn2_w_out=ffn2_w_out,
        pool_w=pool_w.astype(BF16), pool_b=pool_b[:, None, :], pool_scale=pool_scale[:, None, :],
        mlstm_w_in=mlstm_w_in.astype(BF16),
        mlstm_w_gate=jnp.pad(mlstm_w_in[:, :, MLSTM_MAIN:], ((0, 0), (0, 0), (0, GATE_LANES - n_gate))).astype(BF16),
        mlstm_b_gate=jnp.pad(jnp.concatenate([mlstm_b_i, mlstm_b_f], axis=-1),
                             ((0, 0), (0, GATE_LANES - n_gate)))[:, None, :],
        mlstm_norm=mlstm_norm[:, None, :], mlstm_w_out=mlstm_w_out.astype(BF16),
    )
    return _forward(x_prompt, x_sample, state_pool, state_mlstm_C, state_mlstm_n, state_mlstm_m, W)
```

```python
import functools

import jax
import jax.numpy as jnp
from jax import lax
from jax.experimental import pallas as pl
from jax.experimental.pallas import tpu as pltpu

D_MODEL = 1024
D_FF = 2816
DEPTH = 4
PAST_LEN = 1024
POOL_WINDOWS = (2, 4, 8, 16)
POOL_GROUP_WIDTH = D_MODEL // len(POOL_WINDOWS)
POOL_HIST = max(POOL_WINDOWS) - 1
HIST_ROWS = POOL_HIST + 1
MLSTM_HEADS = 4
MLSTM_DQK = D_MODEL // (2 * MLSTM_HEADS)
MLSTM_DV = D_MODEL // MLSTM_HEADS
MLSTM_QK = MLSTM_HEADS * MLSTM_DQK
MLSTM_V = MLSTM_HEADS * MLSTM_DV
MLSTM_MAIN = 2 * MLSTM_QK + 2 * MLSTM_V
GATE_LANES = 128
EPS = 1e-6

VMEM_LIMIT_BYTES = 56 * 1024 * 1024
FFN_SUB_ROWS = 256
MLSTM_SUB_ROWS = 512

F32 = jnp.float32
BF16 = jnp.bfloat16


def _rmsnorm_f32(x, g):
    return x * lax.rsqrt(jnp.mean(x * x, axis=-1, keepdims=True) + EPS) * g


def _resident(shape, layer, col=0):
    idx = (layer,) + (0,) * (len(shape) - 1) + (col,)
    return pl.BlockSpec((None,) + tuple(shape), lambda *_: idx, pipeline_mode=pl.Buffered(1))


def _pool_rows(x, head, ext_ref, pos_first, g, w_ref, b, sc):
    n = x.shape[0]
    u = _rmsnorm_f32(x, g)
    ext_ref[0:HIST_ROWS, :] = head
    ext_ref[HIST_ROWS:HIST_ROWS + n, :] = u
    ext = ext_ref[...]
    pos = pos_first + lax.broadcasted_iota(jnp.int32, (n, 1), 0)
    ys = []
    for gi, w in enumerate(POOL_WINDOWS):
        c0 = gi * POOL_GROUP_WIDTH
        s = ext[:, c0:c0 + POOL_GROUP_WIDTH]
        k = 1
        while k < w:
            s = s + pltpu.roll(s, k, 0)
            k *= 2
        cnt = jnp.minimum(pos + 1, w).astype(F32)
        d = s[HIST_ROWS:, :] / cnt - u[:, c0:c0 + POOL_GROUP_WIDTH]
        ys.append(jnp.dot(d.astype(BF16), w_ref[gi], preferred_element_type=F32))
    y = (jnp.concatenate(ys, axis=-1) + b) * sc
    return x + y, ext_ref[n:n + HIST_ROWS, :]


def _convert_slab(src_ref, dst_ref, in_buf, out_buf, in_sem, out_sem, *, step, n_steps):
    rows = in_buf.shape[1]
    slot = step % 2

    def in_copy(k, s):
        return pltpu.make_async_copy(src_ref.at[pl.ds(k * rows, rows), :], in_buf.at[s], in_sem.at[s])

    def out_copy(k, s):
        return pltpu.make_async_copy(out_buf.at[s], dst_ref.at[pl.ds(k * rows, rows), :], out_sem.at[s])

    @pl.when(step == 0)
    def _():
        in_copy(0, 0).start()

    in_copy(step, slot).wait()

    @pl.when(step + 1 < n_steps)
    def _():
        in_copy(step + 1, 1 - slot).start()

    @pl.when(step >= 2)
    def _():
        out_copy(step - 2, slot).wait()

    out_buf[slot] = in_buf[slot].astype(BF16)

    def finish():
        out_copy(step, slot).start()

        @pl.when(step == n_steps - 1)
        def _():
            if n_steps >= 2:
                out_copy(step - 1, 1 - slot).wait()
            out_copy(step, slot).wait()

    return finish


def _ffn_body(*refs, final_norm, sub, n_steps, convert_layer, pool):
    it = iter(refs)
    take = lambda n: [next(it) for _ in range(n)]
    do_convert = convert_layer is not None
    x_ref, g_ref, wa_ref, wb_ref, wo_ref, gf_ref = take(6)
    hist_ref, gm_ref, pw_ref, pb_ref, ps_ref = take(5) if pool else [None] * 5
    nwi_ref, nwo_ref = take(2) if do_convert else [None] * 2
    (o_ref,) = take(1)
    (hist_out_ref,) = take(1) if pool else [None]
    cwi_ref, cwo_ref = take(2) if do_convert else [None] * 2
    ext_ref, carry_ref = take(2) if pool else [None] * 2
    step = pl.program_id(0)
    tm = x_ref.shape[0]
    if pool:
        tiles_per_seq, pos0 = pool
        t_in_seq = step % tiles_per_seq

        @pl.when(t_in_seq == 0)
        def _():
            carry_ref[...] = hist_ref[0]

    finishers = []
    if do_convert:
        wi_in, wi_out, wo_in, wo_out, sem_ii, sem_io, sem_oi, sem_oo = take(8)
        finishers = [
            _convert_slab(nwi_ref.at[convert_layer], cwi_ref, wi_in, wi_out, sem_ii, sem_io, step=step, n_steps=n_steps),
            _convert_slab(nwo_ref.at[convert_layer], cwo_ref, wo_in, wo_out, sem_oi, sem_oo, step=step, n_steps=n_steps),
        ]

    def pool_and_store(r0, y):
        y, tail = _pool_rows(y, carry_ref[...], ext_ref, pos0 + t_in_seq * tm + r0, gm_ref[...], pw_ref,
                             pb_ref[...], ps_ref[...])
        carry_ref[...] = tail
        hist_out_ref[0] = tail
        o_ref[r0:r0 + sub, :] = y

    deferred = None
    for r0 in range(0, tm, sub):
        x = x_ref[r0:r0 + sub, :]
        xn = _rmsnorm_f32(x, g_ref[...]).astype(BF16)
        a = jnp.dot(xn, wa_ref[...], preferred_element_type=F32)
        b = jnp.dot(xn, wb_ref[...], preferred_element_type=F32)
        h = (a * jax.nn.sigmoid(a) * b).astype(BF16)
        y = x + 0.5 * jnp.dot(h, wo_ref[...], preferred_element_type=F32)
        if final_norm:
            y = _rmsnorm_f32(y, gf_ref[...])
        if pool:
            if deferred is not None:
                pool_and_store(*deferred)
            deferred = (r0, y)
        else:
            o_ref[r0:r0 + sub, :] = y
    if deferred is not None:
        pool_and_store(*deferred)
    for finish in finishers:
        finish()


def _ffn(x, g, w_in, w_out, g_final, *, layer, final_norm, tm, convert=None, pool=None):
    T = x.shape[0]
    assert T % tm == 0
    n_steps = T // tm
    sub = min(tm, FFN_SUB_ROWS)
    row = pl.BlockSpec((tm, D_MODEL), lambda i: (i, 0))
    in_specs = [
        row,
        _resident((1, D_MODEL), layer),
        pl.BlockSpec((D_MODEL, D_FF), lambda i: (0, 0), pipeline_mode=pl.Buffered(1)),
        pl.BlockSpec((D_MODEL, D_FF), lambda i: (0, 1), pipeline_mode=pl.Buffered(1)),
        pl.BlockSpec((D_FF, D_MODEL), lambda i: (0, 0), pipeline_mode=pl.Buffered(1)),
        _resident((1, D_MODEL), 0),
    ]
    out_specs = [row]
    out_shape = [jax.ShapeDtypeStruct((T, D_MODEL), F32)]
    operands = [x, g, w_in, w_in, w_out, g_final]
    scratch = []
    pool_cfg = None
    if pool is not None:
        hist, g_mix, pw, pb, psc, pool_layer, seq, pos0 = pool
        assert seq % tm == 0 and sub >= HIST_ROWS
        tiles_per_seq = seq // tm
        pool_cfg = (tiles_per_seq, pos0)
        in_specs += [
            pl.BlockSpec((None, 1, HIST_ROWS, D_MODEL), lambda i: (pool_layer, i // tiles_per_seq, 0, 0)),
            _resident((1, D_MODEL), layer),
            _resident(pw.shape[1:], pool_layer),
            _resident((1, D_MODEL), pool_layer),
            _resident((1, D_MODEL), pool_layer),
        ]
        out_specs += [pl.BlockSpec((1, HIST_ROWS, D_MODEL), lambda i: (i // tiles_per_seq, 0, 0))]
        out_shape += [jax.ShapeDtypeStruct((T // seq, HIST_ROWS, D_MODEL), F32)]
        operands += [hist, g_mix, pw, pb, psc]
        scratch += [pltpu.VMEM((HIST_ROWS + sub, D_MODEL), F32), pltpu.VMEM((HIST_ROWS, D_MODEL), F32)]
    convert_layer = None
    if convert is not None:
        src_in, src_out, convert_layer = convert
        assert D_MODEL % n_steps == 0 and D_FF % n_steps == 0
        r_in, r_out = D_MODEL // n_steps, D_FF // n_steps
        any_spec = pl.BlockSpec(memory_space=pl.ANY)
        in_specs += [any_spec, any_spec]
        out_specs += [any_spec, any_spec]
        out_shape += [jax.ShapeDtypeStruct((D_MODEL, 2 * D_FF), BF16), jax.ShapeDtypeStruct((D_FF, D_MODEL), BF16)]
        operands += [src_in, src_out]
        scratch += [
            pltpu.VMEM((2, r_in, 2 * D_FF), F32), pltpu.VMEM((2, r_in, 2 * D_FF), BF16),
            pltpu.VMEM((2, r_out, D_MODEL), F32), pltpu.VMEM((2, r_out, D_MODEL), BF16),
        ] + [pltpu.SemaphoreType.DMA((2,))] * 4
    return pl.pallas_call(
        functools.partial(_ffn_body, final_norm=final_norm, sub=sub, n_steps=n_steps,
                          convert_layer=convert_layer, pool=pool_cfg),
        grid=(n_steps,),
        in_specs=in_specs,
        out_specs=out_specs,
        out_shape=out_shape,
        scratch_shapes=scratch,
        compiler_params=pltpu.CompilerParams(
            dimension_semantics=("arbitrary",), vmem_limit_bytes=VMEM_LIMIT_BYTES),
        name="ffn",
    )(*operands)


def _pool_body(x_ref, hist_ref, g_ref, w_ref, b_ref, sc_ref, o_ref, hist_out_ref, ext_ref, carry_ref,
               *, ts, pos0):
    t = pl.program_id(1)

    @pl.when(t == 0)
    def _():
        carry_ref[...] = hist_ref[0]

    y, tail = _pool_rows(x_ref[0], carry_ref[...], ext_ref, pos0 + t * ts, g_ref[...], w_ref, b_ref[...],
                         sc_ref[...])
    o_ref[0] = y
    carry_ref[...] = tail
    hist_out_ref[0] = tail


def _pool_mixer(x, hist, g, w, b, sc, *, layer, pool_layer, pos0, ts):
    B, S, _ = x.shape
    assert S % ts == 0 and ts >= HIST_ROWS
    return pl.pallas_call(
        functools.partial(_pool_body, ts=ts, pos0=pos0),
        grid=(B, S // ts),
        in_specs=[
            pl.BlockSpec((1, ts, D_MODEL), lambda bi, t: (bi, t, 0)),
            pl.BlockSpec((None, 1, HIST_ROWS, D_MODEL), lambda bi, t: (pool_layer, bi, 0, 0)),
            _resident((1, D_MODEL), layer),
            _resident(w.shape[1:], pool_layer),
            _resident((1, D_MODEL), pool_layer),
            _resident((1, D_MODEL), pool_layer),
        ],
        out_specs=[
            pl.BlockSpec((1, ts, D_MODEL), lambda bi, t: (bi, t, 0)),
            pl.BlockSpec((1, HIST_ROWS, D_MODEL), lambda bi, t: (bi, 0, 0)),
        ],
        out_shape=[
            jax.ShapeDtypeStruct((B, S, D_MODEL), F32),
            jax.ShapeDtypeStruct((B, HIST_ROWS, D_MODEL), F32),
        ],
        scratch_shapes=[pltpu.VMEM((HIST_ROWS + ts, D_MODEL), F32), pltpu.VMEM((HIST_ROWS, D_MODEL), F32)],
        compiler_params=pltpu.CompilerParams(
            dimension_semantics=("arbitrary", "arbitrary"), vmem_limit_bytes=VMEM_LIMIT_BYTES),
        name="pool_mixer",
    )(x, hist, g, w, b, sc)


def _log_sigmoid(x):
    return jnp.minimum(x, 0.0) - jnp.log1p(jnp.exp(-jnp.abs(x)))


def _dot_nt(a, b):
    return lax.dot_general(a, b, (((1,), (1,)), ((), ())), preferred_element_type=F32)


def _dot_tn(a, b):
    return lax.dot_general(a, b, (((0,), (0,)), ((), ())), preferred_element_type=F32)


def _mlstm_body(x_ref, c0_ref, n0_ref, m0_ref, g_ref, win_ref, wg_ref, bg_ref, gn_ref, wo_ref,
                o_ref, c_ref, n_ref, m_ref, *scratch, tl, L, per_chunk_state, sub):
    H = MLSTM_HEADS
    t = pl.program_id(1)
    n_sub = tl // sub
    p_ss, gate_ss, hn_ss = scratch[:n_sub], scratch[n_sub:2 * n_sub], scratch[2 * n_sub:]

    @pl.when(t == 0)
    def _():
        c_ref[...] = c0_ref[...]
        n_ref[...] = n0_ref[...]
        m_ref[...] = m0_ref[...]

    for i_sub in range(n_sub):
        x = x_ref[i_sub * sub:(i_sub + 1) * sub, :]
        u = _rmsnorm_f32(x, g_ref[...]).astype(BF16)
        p_ss[i_sub][...] = jnp.dot(u, win_ref[...], preferred_element_type=F32)
        pre = jnp.dot(u, wg_ref[...], preferred_element_type=F32) + bg_ref[...]
        lane = lax.broadcasted_iota(jnp.int32, pre.shape, 1)
        gate_ss[i_sub][...] = jnp.where(lane < H, pre, _log_sigmoid(pre))

    row_g = lax.broadcasted_iota(jnp.int32, (L, GATE_LANES), 0)
    causal = lax.broadcasted_iota(jnp.int32, (L, L), 0) >= lax.broadcasted_iota(jnp.int32, (L, L), 1)
    for c in range(tl // L):
        i_sub, r0 = divmod(c * L, sub)
        p_s, gate_s, hn_s = p_ss[i_sub], gate_ss[i_sub], hn_ss[i_sub]
        si = c if per_chunk_state else 0
        G = gate_s[r0:r0 + L, :]
        Bc = G
        k = 1
        while k < L:
            Bc = Bc + jnp.where(row_g >= k, pltpu.roll(Bc, k, 0), 0.0)
            k *= 2
        A = G - pltpu.roll(Bc, GATE_LANES - H, 1)
        AT = A.T
        for h in range(H):
            bcol = Bc[:, H + h:H + h + 1]
            acol = A[:, h:h + 1]
            arow = AT[h:h + 1, :]
            C = c_ref[si, h]
            n = n_ref[si, h]
            m_prev = m_ref[si, h][:, 0:1]
            q = p_s[r0:r0 + L, h * MLSTM_DQK:(h + 1) * MLSTM_DQK] * (MLSTM_DQK ** -0.5)
            kk = p_s[r0:r0 + L, MLSTM_QK + h * MLSTM_DQK:MLSTM_QK + (h + 1) * MLSTM_DQK]
            v0 = 2 * MLSTM_QK + h * MLSTM_DV
            v = p_s[r0:r0 + L, v0:v0 + MLSTM_DV]
            og = p_s[r0:r0 + L, v0 + MLSTM_V:v0 + MLSTM_V + MLSTM_DV]
            qb = q.astype(BF16)
            kb = kk.astype(BF16)

            log_d = jnp.where(causal, bcol + arow, -jnp.inf)
            log_state = bcol + m_prev
            m_t = jnp.maximum(jnp.max(log_d, axis=-1, keepdims=True), log_state)
            w_intra = jnp.exp(log_d - m_t)
            w_state = jnp.exp(log_state - m_t)
            s = _dot_nt(qb, kb) * w_intra
            num = (jnp.dot(s.astype(BF16), v.astype(BF16), preferred_element_type=F32)
                   + w_state * _dot_nt(qb, C.astype(BF16)))
            den = jnp.sum(s, axis=-1, keepdims=True) + w_state * jnp.sum(q * n, axis=-1, keepdims=True)
            hh = num / jnp.maximum(jnp.abs(den), jnp.exp(-m_t))
            hh = hh * lax.rsqrt(jnp.mean(hh * hh, axis=-1, keepdims=True) + EPS)
            hh = hh * gn_ref[:, h * MLSTM_DV:(h + 1) * MLSTM_DV]
            hn_s[r0:r0 + L, h * MLSTM_DV:(h + 1) * MLSTM_DV] = (jax.nn.sigmoid(og) * hh).astype(BF16)

            m_new = m_t[L - 1:L, :]
            b_last = bcol[L - 1:L, :]
            w_carry = jnp.exp(b_last + m_prev - m_new)
            w_write = jnp.exp(b_last + acol - m_new)
            c_ref[si, h] = w_carry * C + _dot_tn((v * w_write).astype(BF16), kb)
            n_ref[si, h] = w_carry * n + jnp.sum(kk * w_write, axis=0, keepdims=True)
            m_ref[si, h] = jnp.broadcast_to(m_new, (1, GATE_LANES))

        if r0 + L == sub:
            rows = slice(i_sub * sub, (i_sub + 1) * sub)
            o_ref[rows, :] = x_ref[rows, :] + jnp.dot(hn_s[...], wo_ref[...], preferred_element_type=F32)


def _mlstm_mixer(x, C0, n0, m0, g, w_main, w_gate, b_gate, g_norm, w_out,
                 *, layer, cell_layer, seq, tl, L, per_chunk_state):
    T = x.shape[0]
    H = MLSTM_HEADS
    B = C0.shape[1]
    jl = cell_layer
    sub = min(tl, MLSTM_SUB_ROWS)
    n_sub = tl // sub
    nb = B if per_chunk_state else 1
    tiles_per_seq = 1 if per_chunk_state else seq // tl
    if per_chunk_state:
        assert seq == L and tl == nb * L and T == tl
    else:
        assert seq % tl == 0 and T == B * seq
    assert tl % sub == 0 and sub % L == 0
    row = pl.BlockSpec((tl, D_MODEL), lambda b, t: (b * tiles_per_seq + t, 0))
    c_spec = pl.BlockSpec((nb, H, MLSTM_DV, MLSTM_DQK), lambda b, t: (b, 0, 0, 0))
    n_spec = pl.BlockSpec((nb, H, 1, MLSTM_DQK), lambda b, t: (b, 0, 0, 0))
    m_spec = pl.BlockSpec((nb, H, 1, GATE_LANES), lambda b, t: (b, 0, 0, 0))
    return pl.pallas_call(
        functools.partial(_mlstm_body, tl=tl, L=L, per_chunk_state=per_chunk_state, sub=sub),
        grid=(B // nb, tiles_per_seq),
        in_specs=[
            row,
            pl.BlockSpec((None, nb, H, MLSTM_DV, MLSTM_DQK), lambda b, t: (jl, b, 0, 0, 0)),
            pl.BlockSpec((None, nb, H, 1, MLSTM_DQK), lambda b, t: (jl, b, 0, 0, 0)),
            pl.BlockSpec((None, nb, H, 1, GATE_LANES), lambda b, t: (jl, b, 0, 0, 0)),
            _resident((1, D_MODEL), layer),
            _resident((D_MODEL, MLSTM_MAIN), jl),
            _resident((D_MODEL, GATE_LANES), jl),
            _resident((1, GATE_LANES), jl),
            _resident((1, MLSTM_V), jl),
            _resident((MLSTM_V, D_MODEL), jl),
        ],
        out_specs=[row, c_spec, n_spec, m_spec],
        out_shape=[
            jax.ShapeDtypeStruct((T, D_MODEL), F32),
            jax.ShapeDtypeStruct((B, H, MLSTM_DV, MLSTM_DQK), F32),
            jax.ShapeDtypeStruct((B, H, 1, MLSTM_DQK), F32),
            jax.ShapeDtypeStruct((B, H, 1, GATE_LANES), F32),
        ],
        scratch_shapes=([pltpu.VMEM((sub, MLSTM_MAIN), F32)] * n_sub
                        + [pltpu.VMEM((sub, GATE_LANES), F32)] * n_sub
                        + [pltpu.VMEM((sub, MLSTM_V), BF16)] * n_sub),
        compiler_params=pltpu.CompilerParams(
            dimension_semantics=("arbitrary", "arbitrary"), vmem_limit_bytes=VMEM_LIMIT_BYTES),
        name="mlstm_layer",
    )(x, C0, n0, m0, g, w_main, w_gate, b_gate, g_norm, w_out)


def _trunk(x, pool_hist, mC, mn, mm, pos0, W, ffn_w, *, tm, ts, tl, L, per_chunk_state, convert, fuse_pool):
    B, S, _ = x.shape
    pool_new, C_new, n_new, m_new = [], [], [], []
    hist = jnp.pad(pool_hist, ((0, 0), (0, 0), (1, 0), (0, 0)))
    n0 = mn[:, :, :, None, :]
    m0 = jnp.broadcast_to(mm[:, :, :, None, None], mm.shape + (1, GATE_LANES))
    ffn_calls = [(name, i) for i in range(DEPTH) for name in ('ffn1', 'ffn2')]

    def run_ffn(k, x2, pool=None):
        name, layer = ffn_calls[k]
        cv = None
        if convert and k + 1 < len(ffn_calls):
            nxt, nxt_layer = ffn_calls[k + 1]
            cv = (W[nxt + '_w_in'], W[nxt + '_w_out'], nxt_layer)
        out = _ffn(x2, W['norm_' + name], ffn_w[k][0], ffn_w[k][1], W['norm_final'], layer=layer,
                   final_norm=(k == len(ffn_calls) - 1), tm=tm, convert=cv, pool=pool)
        if cv is not None:
            ffn_w.append((out[-2], out[-1]))
        return out[0], (out[1] if pool is not None else None)

    x2 = x.reshape(B * S, D_MODEL)
    for i in range(DEPTH):
        j = i // 2
        if i % 2 == 0 and fuse_pool:
            x2, hist_new = run_ffn(2 * i, x2, pool=(hist, W['norm_mix'], W['pool_w'], W['pool_b'], W['pool_scale'],
                                                    j, S, pos0))
            pool_new.append(hist_new[:, 1:])
        elif i % 2 == 0:
            x2, _ = run_ffn(2 * i, x2)
            x3, hist_new = _pool_mixer(x2.reshape(B, S, D_MODEL), hist, W['norm_mix'], W['pool_w'],
                                       W['pool_b'], W['pool_scale'], layer=i, pool_layer=j, pos0=pos0, ts=ts)
            x2 = x3.reshape(B * S, D_MODEL)
            pool_new.append(hist_new[:, 1:])
        else:
            x2, _ = run_ffn(2 * i, x2)
            x2, C, n, m = _mlstm_mixer(x2, mC, n0, m0, W['norm_mix'], W['mlstm_w_in'], W['mlstm_w_gate'],
                                       W['mlstm_b_gate'], W['mlstm_norm'], W['mlstm_w_out'],
                                       layer=i, cell_layer=j, seq=S, tl=tl, L=L, per_chunk_state=per_chunk_state)
            C_new.append(C)
            n_new.append(n[:, :, 0, :])
            m_new.append(m[:, :, 0, 0])
        x2, _ = run_ffn(2 * i + 1, x2)
    return (x2.reshape(B, S, D_MODEL), jnp.stack(pool_new), jnp.stack(C_new), jnp.stack(n_new),
            jnp.stack(m_new))


def kernel(x_prompt, x_sample, state_pool, state_mlstm_C, state_mlstm_n, state_mlstm_m, norm_ffn1,
           ffn1_w_in, ffn1_w_out, norm_mix, pool_w, pool_b, pool_scale, mlstm_w_in, mlstm_b_i, mlstm_b_f,
           mlstm_norm, mlstm_w_out, norm_ffn2, ffn2_w_in, ffn2_w_out, norm_final):
    n_gate = 2 * MLSTM_HEADS
    W = dict(
        norm_ffn1=norm_ffn1[:, None, :], norm_ffn2=norm_ffn2[:, None, :], norm_mix=norm_mix[:, None, :],
        norm_final=norm_final[None, None, :],
        ffn1_w_in=ffn1_w_in, ffn1_w_out=ffn1_w_out, ffn2_w_in=ffn2_w_in, ffn2_w_out=ffn2_w_out,
        pool_w=pool_w.astype(BF16), pool_b=pool_b[:, None, :], pool_scale=pool_scale[:, None, :],
        mlstm_w_in=mlstm_w_in.astype(BF16),
        mlstm_w_gate=jnp.pad(mlstm_w_in[:, :, MLSTM_MAIN:], ((0, 0), (0, 0), (0, GATE_LANES - n_gate))).astype(BF16),
        mlstm_b_gate=jnp.pad(jnp.concatenate([mlstm_b_i, mlstm_b_f], axis=-1),
                             ((0, 0), (0, GATE_LANES - n_gate)))[:, None, :],
        mlstm_norm=mlstm_norm[:, None, :], mlstm_w_out=mlstm_w_out.astype(BF16),
    )
    B = x_prompt.shape[0]
    zero_pool = jnp.zeros((state_pool.shape[0], B) + state_pool.shape[2:], state_pool.dtype)
    zero_C = jnp.zeros((state_mlstm_C.shape[0], B) + state_mlstm_C.shape[2:], state_mlstm_C.dtype)
    zero_n = jnp.zeros((state_mlstm_n.shape[0], B) + state_mlstm_n.shape[2:], state_mlstm_n.dtype)
    zero_m = jnp.zeros((state_mlstm_m.shape[0], B) + state_mlstm_m.shape[2:], state_mlstm_m.dtype)
    ffn_w = [(ffn1_w_in[0].astype(BF16), ffn1_w_out[0].astype(BF16))]
    y_p, pool_p, C_p, n_p, m_p = _trunk(x_prompt, zero_pool, zero_C, zero_n, zero_m, 0, W, ffn_w,
                                        tm=1024, ts=512, tl=1024, L=256, per_chunk_state=False, convert=True,
                                        fuse_pool=True)
    Bs, Ss, _ = x_sample.shape
    y_s, pool_s, C_s, n_s, m_s = _trunk(x_sample, state_pool, state_mlstm_C, state_mlstm_n, state_mlstm_m,
                                        PAST_LEN, W, ffn_w, tm=Bs * Ss, ts=Ss, tl=Bs * Ss, L=Ss,
                                        per_chunk_state=True, convert=False, fuse_pool=False)
    return (y_p, y_s, pool_p, C_p, n_p, m_p, pool_s, C_s, n_s, m_s)
```
